```python
import jax, jax.numpy as jnp
from jax import lax
import numpy as np

D_MODEL = 1024
BATCH = 8
SEQ = 2048
DEPTH = 1

ML_HEADS = 4
ML_DIM = D_MODEL
ML_HEAD_DIM = ML_DIM // ML_HEADS
ML_CHUNK = 64
CONV_WIDTH = 4
FOX_HEADS = 16
FOX_HEAD_DIM = 64
FOX_DIM = FOX_HEADS * FOX_HEAD_DIM
Q_BLOCK = 128
D_FF = 2816
EPS = 1e-6
SPLIT_SIZES = (ML_DIM, ML_DIM, ML_DIM, ML_DIM, ML_HEADS, ML_HEADS,
               FOX_DIM, FOX_DIM, FOX_DIM, FOX_HEADS, D_MODEL, D_MODEL)
N_IN = sum(SPLIT_SIZES)

kernel_name = 'macaron_mlstm_fox_gated_hybrid'


def _split_points():
    pts, acc = [], 0
    for s in SPLIT_SIZES[:-1]:
        acc += s
        pts.append(acc)
    return pts


def rmsnorm(x, g):
    xf = x.astype(jnp.float32)
    y = xf * lax.rsqrt(jnp.mean(xf * xf, axis=-1, keepdims=True) + EPS)
    return (y * g.astype(jnp.float32)).astype(x.dtype)


def swiglu(h, w_gate, w_up, w_down):
    return (jax.nn.silu(h @ w_gate) * (h @ w_up)) @ w_down


def causal_depthwise_conv(u, w, b):
    T = u.shape[1]
    up = jnp.pad(u, ((0, 0), (CONV_WIDTH - 1, 0), (0, 0)))
    y = b
    for j in range(CONV_WIDTH):
        y = y + up[:, j:j + T] * w[j]
    return y


def to_heads(u, n_heads):
    B, T, C = u.shape
    return u.reshape(B, T, n_heads, C // n_heads).transpose(0, 2, 1, 3)


def from_heads(u):
    B, H, T, Dh = u.shape
    return u.transpose(0, 2, 1, 3).reshape(B, T, H * Dh)


def mlstm_chunkwise(q, k, v, log_i, log_f):
    B, H, T, Dk = q.shape
    Dv = v.shape[-1]
    L = ML_CHUNK
    nc = T // L

    def to_chunks(a):
        return jnp.moveaxis(a.reshape(B, H, nc, L, *a.shape[3:]), 2, 0)

    xs = tuple(to_chunks(a) for a in (q, k, v, log_i, log_f))
    causal = jnp.tril(jnp.ones((L, L), dtype=bool))

    def step(carry, chunk):
        C, n, m = carry
        qx, kx, vx, ix, fx = chunk
        b = jnp.cumsum(fx, axis=-1)
        dmat = jnp.where(causal, b[..., :, None] - b[..., None, :] + ix[..., None, :], -jnp.inf)
        inter = b + m[..., None]
        m_t = jnp.maximum(jnp.max(dmat, axis=-1), inter)
        w_intra = jnp.exp(dmat - m_t[..., None])
        w_inter = jnp.exp(inter - m_t)
        s = jnp.einsum('bhtd,bhsd->bhts', qx, kx) * w_intra
        num = jnp.einsum('bhts,bhsv->bhtv', s, vx) + w_inter[..., None] * jnp.einsum('bhvd,bhtd->bhtv', C, qx)
        den = jnp.sum(s, axis=-1) + w_inter * jnp.einsum('bhd,bhtd->bht', n, qx)
        h = num / jnp.maximum(jnp.abs(den), jnp.exp(-m_t))[..., None]
        b_last = b[..., -1]
        log_w = b_last[..., None] - b + ix
        m_new = jnp.maximum(b_last + m, jnp.max(log_w, axis=-1))
        decay = jnp.exp(b_last + m - m_new)
        w = jnp.exp(log_w - m_new[..., None])
        C_new = decay[..., None, None] * C + jnp.einsum('bhsv,bhsd->bhvd', vx * w[..., None], kx)
        n_new = decay[..., None] * n + jnp.einsum('bhs,bhsd->bhd', w, kx)
        return (C_new, n_new, m_new), h

    init = (jnp.zeros((B, H, Dv, Dk), jnp.float32),
            jnp.zeros((B, H, Dk), jnp.float32),
            jnp.zeros((B, H), jnp.float32))
    _, hc = lax.scan(step, init, xs)
    return jnp.moveaxis(hc, 0, 2).reshape(B, H, T, Dv)


def forgetting_attention(q, k, v, log_f):
    T = q.shape[2]
    c = jnp.cumsum(log_f, axis=-1)
    scale = FOX_HEAD_DIM ** -0.5
    outs = []
    for blk in range(T // Q_BLOCK):
        q0 = blk * Q_BLOCK
        q1 = q0 + Q_BLOCK
        logits = (jnp.einsum('bhtd,bhsd->bhts', q[:, :, q0:q1], k[:, :, :q1]) * scale
                  + c[:, :, q0:q1, None] - c[:, :, None, :q1])
        mask = (q0 + jnp.arange(Q_BLOCK))[:, None] >= jnp.arange(q1)[None, :]
        p = jax.nn.softmax(jnp.where(mask, logits, -jnp.inf), axis=-1)
        outs.append(jnp.einsum('bhts,bhsd->bhtd', p, v[:, :, :q1]))
    return jnp.concatenate(outs, axis=2)


def headwise_layernorm(h, g):
    mu = jnp.mean(h, axis=-1, keepdims=True)
    var = jnp.mean(jnp.square(h - mu), axis=-1, keepdims=True)
    hn = (h - mu) * lax.rsqrt(var + EPS)
    return from_heads(hn) * g.astype(jnp.float32)


def setup_inputs(seed: int = 0) -> dict:
    key = jax.random.key(seed)
    ks = jax.random.split(key, 20)
    f32 = jnp.float32
    d = D_MODEL

    def w(k, shape, fan_in):
        return jax.random.normal(k, shape, f32) * fan_in ** -0.5

    def gain(k, shape):
        return 1.0 + 0.02 * jax.random.normal(k, shape, f32)

    x = jax.random.normal(ks[0], (BATCH, SEQ, d), f32)
    pts = [0] + _split_points()
    ml_f0 = pts[5]
    fx_f0 = pts[9]
    b_in = 0.02 * jax.random.normal(ks[1], (DEPTH, N_IN), f32)
    b_in = b_in.at[:, ml_f0:ml_f0 + ML_HEADS].add(jnp.linspace(3.0, 6.0, ML_HEADS, dtype=f32))
    b_in = b_in.at[:, fx_f0:fx_f0 + FOX_HEADS].add(jnp.linspace(1.0, 4.0, FOX_HEADS, dtype=f32))
    return {
        'x': x,
        'ffn1_norm': gain(ks[2], (DEPTH, d)),
        'ffn1_w_gate': w(ks[3], (DEPTH, d, D_FF), d),
        'ffn1_w_up': w(ks[4], (DEPTH, d, D_FF), d),
        'ffn1_w_down': w(ks[5], (DEPTH, D_FF, d), D_FF),
        'mix_norm': gain(ks[6], (DEPTH, d)),
        'w_in': w(ks[7], (DEPTH, d, N_IN), d),
        'b_in': b_in,
        'conv_w': w(ks[8], (DEPTH, CONV_WIDTH, 2 * ML_DIM), CONV_WIDTH),
        'conv_b': 0.02 * jax.random.normal(ks[9], (DEPTH, 2 * ML_DIM), f32),
        'ml_head_norm': gain(ks[10], (DEPTH, ML_DIM)),
        'w_out': w(ks[11], (DEPTH, d, d), d),
        'ffn2_norm': gain(ks[12], (DEPTH, d)),
        'ffn2_w_gate': w(ks[13], (DEPTH, d, D_FF), d),
        'ffn2_w_up': w(ks[14], (DEPTH, d, D_FF), d),
        'ffn2_w_down': w(ks[15], (DEPTH, D_FF, d), D_FF),
        'final_norm': gain(ks[16], (d,)),
    }


def reference(x, ffn1_norm, ffn1_w_gate, ffn1_w_up, ffn1_w_down, mix_norm, w_in, b_in,
              conv_w, conv_b, ml_head_norm, w_out, ffn2_norm, ffn2_w_gate, ffn2_w_up,
              ffn2_w_down, final_norm):
    f32 = jnp.float32
    for l in range(DEPTH):
        x = x + 0.5 * swiglu(rmsnorm(x, ffn1_norm[l]), ffn1_w_gate[l], ffn1_w_up[l], ffn1_w_down[l])

        h = rmsnorm(x, mix_norm[l])
        proj = (h @ w_in[l] + b_in[l]).astype(f32)
        (ml_q, ml_k, ml_v, ml_o, ml_i, ml_f,
         fx_q, fx_k, fx_v, fx_f, g_a, g_b) = jnp.split(proj, _split_points(), axis=-1)

        qk = jax.nn.silu(causal_depthwise_conv(jnp.concatenate([ml_q, ml_k], axis=-1),
                                               conv_w[l].astype(f32), conv_b[l].astype(f32)))
        q_a = to_heads(qk[..., :ML_DIM], ML_HEADS)
        k_a = to_heads(qk[..., ML_DIM:], ML_HEADS) * ML_HEAD_DIM ** -0.5
        v_a = to_heads(ml_v, ML_HEADS)
        log_i = ml_i.transpose(0, 2, 1)
        log_f_a = jax.nn.log_sigmoid(ml_f).transpose(0, 2, 1)
        h_a = mlstm_chunkwise(q_a, k_a, v_a, log_i, log_f_a)
        y_a = jax.nn.sigmoid(ml_o) * headwise_layernorm(h_a, ml_head_norm[l])

        log_f_b = jax.nn.log_sigmoid(fx_f).transpose(0, 2, 1)
        h_b = forgetting_attention(to_heads(fx_q, FOX_HEADS), to_heads(fx_k, FOX_HEADS),
                                   to_heads(fx_v, FOX_HEADS), log_f_b)
        y_b = from_heads(h_b)

        y = jax.nn.sigmoid(g_a) * y_a + jax.nn.sigmoid(g_b) * y_b
        x = x + y.astype(x.dtype) @ w_out[l]

        x = x + 0.5 * swiglu(rmsnorm(x, ffn2_norm[l]), ffn2_w_gate[l], ffn2_w_up[l], ffn2_w_down[l])
    return rmsnorm(x, final_norm)
```

```python
import functools

import jax
import jax.numpy as jnp
from jax import lax
from jax.experimental import pallas as pl
from jax.experimental.pallas import tpu as pltpu

D_MODEL = 1024
ML_HEADS = 4
ML_HEAD_DIM = 256
CONV_WIDTH = 4
FOX_HEADS = 16
FOX_HEAD_DIM = 64
D_FF = 2816
EPS = 1e-6

LANES = 128
GATE_COLS = LANES
ML_I0, ML_F0, FX_F0 = 0, ML_HEADS, 2 * ML_HEADS

FFN_TM = 512
FFN_TF = 256
PROJ_TM = 512
PROJ_TN = 1024
ML_CHUNK = 256
HALO = 8
FOX_TQ = 256
FOX_TK = 512
CS_BLK = 256
VMEM_LIMIT = 56 * 1024 * 1024

F32 = jnp.float32
BF16 = jnp.bfloat16


def _dot(a, b):
    return jnp.dot(a, b, preferred_element_type=F32)


def _dot_nt(a, b):
    return lax.dot_general(a, b, (((1,), (1,)), ((), ())), preferred_element_type=F32)


def _dot_tn(a, b):
    return lax.dot_general(a, b, (((0,), (0,)), ((), ())), preferred_element_type=F32)


def _rmsnorm(x, g):
    return x * lax.rsqrt(jnp.mean(x * x, axis=-1, keepdims=True) + EPS) * g


def _sigmoid(x):
    return 1.0 / (1.0 + jnp.exp(-x))


def _log_sigmoid(x):
    return jnp.minimum(x, 0.0) - jnp.log(1.0 + jnp.exp(-jnp.abs(x)))


def _cumsum_rows(x, tril):
    hi = x.astype(BF16)
    r1 = x - hi.astype(F32)
    mid = r1.astype(BF16)
    lo = (r1 - mid.astype(F32)).astype(BF16)
    return _dot(tril, hi) + _dot(tril, mid) + _dot(tril, lo)


def _tril(n):
    r = lax.broadcasted_iota(jnp.int32, (n, n), 0)
    c = lax.broadcasted_iota(jnp.int32, (n, n), 1)
    return r >= c


def _ffn_kernel(*refs, pre_proj, final_norm):
    refs = list(refs)
    x_ref = refs.pop(0)
    if pre_proj:
        za_ref, zb_ref, wo_ref = refs.pop(0), refs.pop(0), refs.pop(0)
    g_ref, wg_ref, wu_ref, wd_ref = refs.pop(0), refs.pop(0), refs.pop(0), refs.pop(0)
    if final_norm:
        gf_ref = refs.pop(0)
    o_ref, a_scr = refs

    x = x_ref[...]
    if pre_proj:
        y = (za_ref[...] + zb_ref[...]).astype(BF16)
        x = x + _dot(y, wo_ref[...])
    h = _rmsnorm(x, g_ref[...]).astype(BF16)
    for j in range(D_FF // FFN_TF):
        cols = slice(j * FFN_TF, (j + 1) * FFN_TF)
        gate = _dot(h, wg_ref[:, cols])
        up = _dot(h, wu_ref[:, cols])
        a_scr[:, cols] = (gate * _sigmoid(gate) * up).astype(BF16)
    out = x + 0.5 * _dot(a_scr[...], wd_ref[...])
    if final_norm:
        out = _rmsnorm(out, gf_ref[...])
    o_ref[...] = out


def _ffn_call(x, norm_g, wg, wu, wd, pre=None, final_g=None):
    n = x.shape[0]
    row = pl.BlockSpec((FFN_TM, D_MODEL), lambda i: (i, 0))
    vec = pl.BlockSpec((1, D_MODEL), lambda i: (0, 0))

    def resident(shape):
        return pl.BlockSpec(shape, lambda i: (0, 0), pipeline_mode=pl.Buffered(1))

    args, specs = [x], [row]
    if pre is not None:
        za, zb, wo = pre
        args += [za, zb, wo]
        specs += [row, row, resident((D_MODEL, D_MODEL))]
    args += [norm_g, wg, wu, wd]
    specs += [vec, resident((D_MODEL, D_FF)), resident((D_MODEL, D_FF)), resident((D_FF, D_MODEL))]
    if final_g is not None:
        args.append(final_g)
        specs.append(vec)
    return pl.pallas_call(
        functools.partial(_ffn_kernel, pre_proj=pre is not None, final_norm=final_g is not None),
        grid=(n // FFN_TM,),
        in_specs=specs,
        out_specs=row,
        out_shape=jax.ShapeDtypeStruct((n, D_MODEL), F32),
        scratch_shapes=[pltpu.VMEM((FFN_TM, D_FF), BF16)],
        compiler_params=pltpu.CompilerParams(
            dimension_semantics=("parallel",), vmem_limit_bytes=VMEM_LIMIT),
        name="ffn2" if pre is not None else "ffn1",
    )(*args)


def _proj_kernel(x_ref, g_ref, w_ref, b_ref, wgt_ref, bgt_ref, p32_ref, p16_ref, gt_ref, h_scr, *, n32):
    j = pl.program_id(1)

    @pl.when(j == 0)
    def _():
        h = _rmsnorm(x_ref[...], g_ref[...]).astype(BF16)
        h_scr[...] = h
        gt_ref[...] = _dot(h, wgt_ref[...]) + bgt_ref[...]

    res = _dot(h_scr[...], w_ref[...]) + b_ref[...]

    @pl.when(j < n32)
    def _():
        p32_ref[...] = res

    @pl.when(j >= n32)
    def _():
        p16_ref[...] = res.astype(BF16)


def _proj_call(x1, norm_g, w, b, w_gates, b_gates, n32, n16):
    n = x1.shape[0]
    nj = n32 + n16
    return pl.pallas_call(
        functools.partial(_proj_kernel, n32=n32),
        grid=(n // PROJ_TM, nj),
        in_specs=[
            pl.BlockSpec((PROJ_TM, D_MODEL), lambda i, j: (i, 0)),
            pl.BlockSpec((1, D_MODEL), lambda i, j: (0, 0)),
            pl.BlockSpec((D_MODEL, PROJ_TN), lambda i, j: (0, j)),
            pl.BlockSpec((1, PROJ_TN), lambda i, j: (0, j)),
            pl.BlockSpec((D_MODEL, GATE_COLS), lambda i, j: (0, 0)),
            pl.BlockSpec((1, GATE_COLS), lambda i, j: (0, 0)),
        ],
        out_specs=[
            pl.BlockSpec((PROJ_TM, PROJ_TN), lambda i, j: (i, jnp.minimum(j, n32 - 1))),
            pl.BlockSpec((PROJ_TM, PROJ_TN), lambda i, j: (i, jnp.maximum(j - n32, 0))),
            pl.BlockSpec((PROJ_TM, GATE_COLS), lambda i, j: (i, 0)),
        ],
        out_shape=[
            jax.ShapeDtypeStruct((n, n32 * PROJ_TN), F32),
            jax.ShapeDtypeStruct((n, n16 * PROJ_TN), BF16),
            jax.ShapeDtypeStruct((n, GATE_COLS), F32),
        ],
        scratch_shapes=[pltpu.VMEM((PROJ_TM, D_MODEL), BF16)],
        compiler_params=pltpu.CompilerParams(
            dimension_semantics=("parallel", "arbitrary"), vmem_limit_bytes=VMEM_LIMIT),
        name="in_proj",
    )(x1, norm_g, w, b, w_gates, b_gates)


def _mlstm_kernel(qk_ref, o_ref, ga_ref, v_ref, gt_ref, cw_ref, cb_ref, ln_ref, z_ref,
                  ext_scr, c_scr, n_scr, m_scr):
    L = ML_CHUNK
    dh = ML_HEAD_DIM

    @pl.when(pl.program_id(1) == 0)
    def _():
        ext_scr[0:HALO, :] = jnp.zeros((HALO, 2 * D_MODEL), F32)
        c_scr[...] = jnp.zeros_like(c_scr)
        n_scr[...] = jnp.zeros_like(n_scr)
        m_scr[...] = jnp.zeros_like(m_scr)

    u = qk_ref[...]
    ext_scr[HALO:HALO + L, :] = u
    y = cb_ref[...] + cw_ref[CONV_WIDTH - 1:CONV_WIDTH, :] * u
    for d in range(1, CONV_WIDTH):
        y = y + cw_ref[CONV_WIDTH - 1 - d:CONV_WIDTH - d, :] * ext_scr[HALO - d:HALO - d + L, :]
    ext_scr[0:HALO, :] = u[L - HALO:L, :]
    qk = y * _sigmoid(y)

    gates = gt_ref[...]
    tril = _tril(L)
    bcum = _cumsum_rows(_log_sigmoid(gates), tril.astype(BF16))
    gates_t = gates.T
    bcum_t = bcum.T

    for h in range(ML_HEADS):
        cols = slice(h * dh, (h + 1) * dh)
        q = qk[:, cols]
        k = qk[:, D_MODEL + h * dh:D_MODEL + (h + 1) * dh] * (dh ** -0.5)
        v = v_ref[:, cols]
        li_col = gates[:, ML_I0 + h:ML_I0 + h + 1]
        li_row = gates_t[ML_I0 + h:ML_I0 + h + 1, :]
        b_col = bcum[:, ML_F0 + h:ML_F0 + h + 1]
        b_row = bcum_t[ML_F0 + h:ML_F0 + h + 1, :]
        m_prev = m_scr[h][:, 0:1]
        c_t = c_scr[h]
        n_row = n_scr[h]

        dmat = jnp.where(tril, b_col - b_row + li_row, -jnp.inf)
        inter = b_col + m_prev
        m_t = jnp.maximum(jnp.max(dmat, axis=1, keepdims=True), inter)
        w_intra = jnp.exp(dmat - m_t)
        w_inter = jnp.exp(inter - m_t)

        q16 = q.astype(BF16)
        k16 = k.astype(BF16)
        s = _dot_nt(q16, k16) * w_intra
        num = _dot(s.astype(BF16), v) + w_inter * _dot(q16, c_t.astype(BF16))
        den = jnp.sum(s, axis=1, keepdims=True) + w_inter * jnp.sum(q * n_row, axis=1, keepdims=True)
        hh = num / jnp.maximum(jnp.abs(den), jnp.exp(-m_t))

        b_last = b_col[L - 1:L, :]
        log_w = b_last - b_col + li_col
        m_new = jnp.maximum(b_last + m_prev, jnp.max(log_w, axis=0, keepdims=True))
        decay = jnp.exp(b_last + m_prev - m_new)
        w = jnp.exp(log_w - m_new)
        vw = (v.astype(F32) * w).astype(BF16)
        c_scr[h] = decay * c_t + _dot_tn(k16, vw)
        n_scr[h] = decay * n_row + jnp.sum(w * k, axis=0, keepdims=True)
        m_scr[h] = jnp.broadcast_to(m_new, (1, LANES))

        mu = jnp.mean(hh, axis=1, keepdims=True)
        dlt = hh - mu
        var = jnp.mean(dlt * dlt, axis=1, keepdims=True)
        hn = dlt * lax.rsqrt(var + EPS) * ln_ref[:, cols]
        z_ref[:, cols] = _sigmoid(ga_ref[:, cols]) * (_sigmoid(o_ref[:, cols]) * hn)


def _mlstm_call(p32, p16, gates, conv_w, conv_b, ln_g, batch, seq):
    n = batch * seq
    nc = seq // ML_CHUNK
    L = ML_CHUNK

    def row(width, col):
        return pl.BlockSpec((L, width), lambda b, c: (b * nc + c, col))

    def full(shape):
        return pl.BlockSpec(shape, lambda b, c: (0, 0))

    return pl.pallas_call(
        _mlstm_kernel,
        grid=(batch, nc),
        in_specs=[
            row(2 * D_MODEL, 0),
            row(D_MODEL, 2),
            row(D_MODEL, 3),
            row(D_MODEL, 0),
            row(GATE_COLS, 0),
            full((CONV_WIDTH, 2 * D_MODEL)),
            full((1, 2 * D_MODEL)),
            full((1, D_MODEL)),
        ],
        out_specs=row(D_MODEL, 0),
        out_shape=jax.ShapeDtypeStruct((n, D_MODEL), F32),
        scratch_shapes=[
            pltpu.VMEM((HALO + L, 2 * D_MODEL), F32),
            pltpu.VMEM((ML_HEADS, ML_HEAD_DIM, ML_HEAD_DIM), F32),
            pltpu.VMEM((ML_HEADS, 1, ML_HEAD_DIM), F32),
            pltpu.VMEM((ML_HEADS, 1, LANES), F32),
        ],
        compiler_params=pltpu.CompilerParams(
            dimension_semantics=("parallel", "arbitrary"), vmem_limit_bytes=VMEM_LIMIT),
        name="mlstm",
    )(p32, p32, p32, p16, gates, conv_w, conv_b, ln_g)


def _fox_kernel(q_ref, k_ref, v_ref, gb_ref, gt_ref, z_ref, ccol_scr, crow_scr, *, seq):
    tq, tk = FOX_TQ, FOX_TK
    pair = pl.program_id(1)

    @pl.when(pair == 0)
    def _():
        tril = _tril(CS_BLK).astype(BF16)
        carry = jnp.zeros((1, GATE_COLS), F32)
        per_k = tk // CS_BLK
        for blk in range(seq // CS_BLK):
            rows = slice(blk * CS_BLK, (blk + 1) * CS_BLK)
            cs = _cumsum_rows(_log_sigmoid(gt_ref[rows, :]), tril) + carry
            ccol_scr[rows, :] = cs
            crow_scr[blk // per_k, :, (blk % per_k) * CS_BLK:(blk % per_k + 1) * CS_BLK] = cs.T
            carry = cs[CS_BLK - 1:CS_BLK, :]

    lane = lax.broadcasted_iota(jnp.int32, (1, LANES), 1)
    low = lane < FOX_HEAD_DIM
    row_id = lax.broadcasted_iota(jnp.int32, (tq, tk), 0)
    col_id = lax.broadcasted_iota(jnp.int32, (tq, tk), 1)
    gate_lane = lax.broadcasted_iota(jnp.int32, (1, GATE_COLS), 1)
    scale = FOX_HEAD_DIM ** -0.5

    def q_block(qi, _):
        q0 = pl.multiple_of(qi * tq, tq)
        qp = q_ref[pl.ds(q0, tq), :].astype(F32)
        ccol = ccol_scr[pl.ds(q0, tq), :]
        n_kb = (q0 + tq + tk - 1) // tk
        outs = []
        for e in range(2):
            head_col = FX_F0 + 2 * pair + e
            qe = (qp * jnp.where(low if e == 0 else ~low, scale, 0.0)).astype(BF16)
            cq = jnp.sum(jnp.where(gate_lane == head_col, ccol, 0.0), axis=1, keepdims=True)

            def kv_step(kb, carry, masked, qe=qe, cq=cq, head_col=head_col):
                m, l, acc = carry
                k0 = pl.multiple_of(kb * tk, tk)
                kblk = k_ref[pl.ds(k0, tk), :]
                vblk = v_ref[pl.ds(k0, tk), :]
                ck = crow_scr[kb, pl.ds(head_col, 1), :]
                logit = _dot_nt(qe, kblk) + (cq - ck)
                if masked:
                    logit = jnp.where(q0 + row_id >= k0 + col_id, logit, -jnp.inf)
                m_new = jnp.maximum(m, jnp.max(logit, axis=1, keepdims=True))
                alpha = jnp.exp(m - m_new)
                p = jnp.exp(logit - m_new)
                l = alpha * l + jnp.sum(p, axis=1, keepdims=True)
                acc = alpha * acc + _dot(p.astype(BF16), vblk)
                return m_new, l, acc

            init = (jnp.full((tq, 1), -jnp.inf, F32), jnp.zeros((tq, 1), F32), jnp.zeros((tq, LANES), F32))
            carry = lax.fori_loop(0, n_kb - 1, functools.partial(kv_step, masked=False), init)
            _, l, acc = kv_step(n_kb - 1, carry, masked=True)
            outs.append(acc / l)
        o = jnp.where(low, outs[0], outs[1])
        z_ref[pl.ds(q0, tq), :] = _sigmoid(gb_ref[pl.ds(q0, tq), :]) * o
        return 0

    lax.fori_loop(0, seq // tq, q_block, 0)


def _fox_call(p32, p16, gates, batch, seq):
    n = batch * seq
    pairs = FOX_HEADS // 2
    per_group = D_MODEL // LANES

    def cols(group):
        return pl.BlockSpec((seq, LANES), lambda b, p: (b, group * per_group + p))

    return pl.pallas_call(
        functools.partial(_fox_kernel, seq=seq),
        grid=(batch, pairs),
        in_specs=[
            cols(1),
            cols(2),
            cols(3),
            cols(4),
            pl.BlockSpec((seq, GATE_COLS), lambda b, p: (b, 0)),
        ],
        out_specs=pl.BlockSpec((seq, LANES), lambda b, p: (b, p)),
        out_shape=jax.ShapeDtypeStruct((n, D_MODEL), F32),
        scratch_shapes=[
            pltpu.VMEM((seq, GATE_COLS), F32),
            pltpu.VMEM((seq // FOX_TK, GATE_COLS, FOX_TK), F32),
        ],
        compiler_params=pltpu.CompilerParams(
            dimension_semantics=("parallel", "arbitrary"), vmem_limit_bytes=VMEM_LIMIT),
        name="fox",
    )(p16, p16, p16, p32, gates)


def _layer(x2d, batch, seq, ffn1_norm, ffn1_w_gate, ffn1_w_up, ffn1_w_down, mix_norm, w_in, b_in,
           conv_w, conv_b, ml_head_norm, w_out, ffn2_norm, ffn2_w_gate, ffn2_w_up, ffn2_w_down, final_g):
    d = D_MODEL
    x1 = _ffn_call(x2d, ffn1_norm[None, :], ffn1_w_gate.astype(BF16), ffn1_w_up.astype(BF16),
                   ffn1_w_down.astype(BF16))

    sizes = (d, d, d, d, ML_HEADS, ML_HEADS, d, d, d, FOX_HEADS, d, d)
    starts = [0]
    for s in sizes[:-1]:
        starts.append(starts[-1] + s)
    (ml_q, ml_k, ml_v, ml_o, ml_i, ml_f, fx_q, fx_k, fx_v, fx_f, g_a, g_b) = [
        slice(a, a + s) for a, s in zip(starts, sizes)]
    order32 = (ml_q, ml_k, ml_o, g_a, g_b)
    order16 = (ml_v, fx_q, fx_k, fx_v)
    w_main = jnp.concatenate([w_in[:, c] for c in order32 + order16], axis=1).astype(BF16)
    b_main = jnp.concatenate([b_in[c] for c in order32 + order16])[None, :]
    pad = GATE_COLS - 2 * ML_HEADS - FOX_HEADS
    w_gates = jnp.concatenate([w_in[:, ml_i], w_in[:, ml_f], w_in[:, fx_f],
                               jnp.zeros((d, pad), w_in.dtype)], axis=1).astype(BF16)
    b_gates = jnp.concatenate([b_in[ml_i], b_in[ml_f], b_in[fx_f], jnp.zeros((pad,), b_in.dtype)])[None, :]
    p32, p16, gates = _proj_call(x1, mix_norm[None, :], w_main, b_main, w_gates, b_gates,
                                 len(order32), len(order16))

    z_a = _mlstm_call(p32, p16, gates, conv_w, conv_b[None, :], ml_head_norm[None, :], batch, seq)
    z_b = _fox_call(p32, p16, gates, batch, seq)

    return _ffn_call(x1, ffn2_norm[None, :], ffn2_w_gate.astype(BF16), ffn2_w_up.astype(BF16),
                     ffn2_w_down.astype(BF16), pre=(z_a, z_b, w_out.astype(BF16)), final_g=final_g)


def kernel(x, ffn1_norm, ffn1_w_gate, ffn1_w_up, ffn1_w_down, mix_norm, w_in, b_in, conv_w, conv_b,
           ml_head_norm, w_out, ffn2_norm, ffn2_w_gate, ffn2_w_up, ffn2_w_down, final_norm):
    batch, seq, d = x.shape
    assert d == D_MODEL and ffn1_norm.shape[0] == 1, "single-layer configuration"
    out = _layer(x.reshape(batch * seq, d), batch, seq, ffn1_norm[0], ffn1_w_gate[0], ffn1_w_up[0],
                 ffn1_w_down[0], mix_norm[0], w_in[0], b_in[0], conv_w[0], conv_b[0], ml_head_norm[0],
                 w_out[0], ffn2_norm[0], ffn2_w_gate[0], ffn2_w_up[0], ffn2_w_down[0], final_norm[None, :])
    return out.reshape(batch, seq, d)
```

```python
import functools

import jax
import jax.numpy as jnp
from jax import lax
from jax.experimental import pallas as pl
from jax.experimental.pallas import tpu as pltpu

D_MODEL = 1024
ML_HEADS = 4
ML_HEAD_DIM = 256
CONV_WIDTH = 4
FOX_HEADS = 16
FOX_HEAD_DIM = 64
D_FF = 2816
EPS = 1e-6

LANES = 128
GATE_COLS = LANES
ML_I0, ML_F0, FX_F0 = 0, ML_HEADS, 2 * ML_HEADS

FFN_TM = 512
FFN_TF = 256
PROJ_TM = 512
PROJ_TN = 1024
ML_CHUNK = 256
HALO = 8
FOX_TQ = 256
FOX_FEAT = 8
FOX_VROWS = 80
CS_BLK = 256
VMEM_LIMIT = 56 * 1024 * 1024

F32 = jnp.float32
BF16 = jnp.bfloat16


def _dot(a, b):
    return jnp.dot(a, b, preferred_element_type=F32)


def _dot_nt(a, b):
    return lax.dot_general(a, b, (((1,), (1,)), ((), ())), preferred_element_type=F32)


def _dot_tn(a, b):
    return lax.dot_general(a, b, (((0,), (0,)), ((), ())), preferred_element_type=F32)


def _rmsnorm(x, g):
    return x * lax.rsqrt(jnp.mean(x * x, axis=-1, keepdims=True) + EPS) * g


def _sigmoid(x):
    return 1.0 / (1.0 + jnp.exp(-x))


def _log_sigmoid(x):
    return jnp.minimum(x, 0.0) - jnp.log(1.0 + jnp.exp(-jnp.abs(x)))


def _cumsum_rows(x, tril):
    hi = x.astype(BF16)
    r1 = x - hi.astype(F32)
    mid = r1.astype(BF16)
    lo = (r1 - mid.astype(F32)).astype(BF16)
    return _dot(tril, hi) + _dot(tril, mid) + _dot(tril, lo)


def _tril(n):
    r = lax.broadcasted_iota(jnp.int32, (n, n), 0)
    c = lax.broadcasted_iota(jnp.int32, (n, n), 1)
    return r >= c


def _ffn_kernel(*refs, pre_proj, final_norm):
    refs = list(refs)
    x_ref = refs.pop(0)
    if pre_proj:
        za_ref, zb_ref, wo_ref = refs.pop(0), refs.pop(0), refs.pop(0)
    g_ref, wg_ref, wu_ref, wd_ref = refs.pop(0), refs.pop(0), refs.pop(0), refs.pop(0)
    if final_norm:
        gf_ref = refs.pop(0)
    o_ref, a_scr = refs

    x = x_ref[...]
    if pre_proj:
        y = (za_ref[...] + zb_ref[...]).astype(BF16)
        x = x + _dot(y, wo_ref[...])
    h = _rmsnorm(x, g_ref[...]).astype(BF16)
    for j in range(D_FF // FFN_TF):
        cols = slice(j * FFN_TF, (j + 1) * FFN_TF)
        gate = _dot(h, wg_ref[:, cols])
        up = _dot(h, wu_ref[:, cols])
        a_scr[:, cols] = (gate * _sigmoid(gate) * up).astype(BF16)
    out = x + 0.5 * _dot(a_scr[...], wd_ref[...])
    if final_norm:
        out = _rmsnorm(out, gf_ref[...])
    o_ref[...] = out


def _ffn_call(x, norm_g, wg, wu, wd, pre=None, final_g=None):
    n = x.shape[0]
    row = pl.BlockSpec((FFN_TM, D_MODEL), lambda i: (i, 0))
    vec = pl.BlockSpec((1, D_MODEL), lambda i: (0, 0))

    def resident(shape):
        return pl.BlockSpec(shape, lambda i: (0, 0), pipeline_mode=pl.Buffered(1))

    args, specs = [x], [row]
    if pre is not None:
        za, zb, wo = pre
        args += [za, zb, wo]
        specs += [row, row, resident((D_MODEL, D_MODEL))]
    args += [norm_g, wg, wu, wd]
    specs += [vec, resident((D_MODEL, D_FF)), resident((D_MODEL, D_FF)), resident((D_FF, D_MODEL))]
    if final_g is not None:
        args.append(final_g)
        specs.append(vec)
    return pl.pallas_call(
        functools.partial(_ffn_kernel, pre_proj=pre is not None, final_norm=final_g is not None),
        grid=(n // FFN_TM,),
        in_specs=specs,
        out_specs=row,
        out_shape=jax.ShapeDtypeStruct((n, D_MODEL), F32),
        scratch_shapes=[pltpu.VMEM((FFN_TM, D_FF), BF16)],
        compiler_params=pltpu.CompilerParams(
            dimension_semantics=("parallel",), vmem_limit_bytes=VMEM_LIMIT),
        name="ffn2" if pre is not None else "ffn1",
    )(*args)


def _proj_kernel(x_ref, g_ref, w_ref, b_ref, wgt_ref, bgt_ref, p32_ref, p16_ref, gt_ref, h_scr, *, n32):
    j = pl.program_id(1)

    @pl.when(j == 0)
    def _():
        h = _rmsnorm(x_ref[...], g_ref[...]).astype(BF16)
        h_scr[...] = h
        gt_ref[...] = _dot(h, wgt_ref[...]) + bgt_ref[...]

    res = _dot(h_scr[...], w_ref[...]) + b_ref[...]

    @pl.when(j < n32)
    def _():
        p32_ref[...] = res

    @pl.when(j >= n32)
    def _():
        p16_ref[...] = res.astype(BF16)


def _proj_call(x1, norm_g, w, b, w_gates, b_gates, n32, n16):
    n = x1.shape[0]
    nj = n32 + n16
    return pl.pallas_call(
        functools.partial(_proj_kernel, n32=n32),
        grid=(n // PROJ_TM, nj),
        in_specs=[
            pl.BlockSpec((PROJ_TM, D_MODEL), lambda i, j: (i, 0)),
            pl.BlockSpec((1, D_MODEL), lambda i, j: (0, 0)),
            pl.BlockSpec((D_MODEL, PROJ_TN), lambda i, j: (0, j)),
            pl.BlockSpec((1, PROJ_TN), lambda i, j: (0, j)),
            pl.BlockSpec((D_MODEL, GATE_COLS), lambda i, j: (0, 0)),
            pl.BlockSpec((1, GATE_COLS), lambda i, j: (0, 0)),
        ],
        out_specs=[
            pl.BlockSpec((PROJ_TM, PROJ_TN), lambda i, j: (i, jnp.minimum(j, n32 - 1))),
            pl.BlockSpec((PROJ_TM, PROJ_TN), lambda i, j: (i, jnp.maximum(j - n32, 0))),
            pl.BlockSpec((PROJ_TM, GATE_COLS), lambda i, j: (i, 0)),
        ],
        out_shape=[
            jax.ShapeDtypeStruct((n, n32 * PROJ_TN), F32),
            jax.ShapeDtypeStruct((n, n16 * PROJ_TN), BF16),
            jax.ShapeDtypeStruct((n, GATE_COLS), F32),
        ],
        scratch_shapes=[pltpu.VMEM((PROJ_TM, D_MODEL), BF16)],
        compiler_params=pltpu.CompilerParams(
            dimension_semantics=("parallel", "arbitrary"), vmem_limit_bytes=VMEM_LIMIT),
        name="in_proj",
    )(x1, norm_g, w, b, w_gates, b_gates)


def _mlstm_kernel(qk_ref, o_ref, ga_ref, v_ref, gt_ref, cw_ref, cb_ref, ln_ref, z_ref,
                  ext_scr, c_scr, n_scr, m_scr):
    L = ML_CHUNK
    dh = ML_HEAD_DIM

    @pl.when(pl.program_id(1) == 0)
    def _():
        ext_scr[0:HALO, :] = jnp.zeros((HALO, 2 * D_MODEL), F32)
        c_scr[...] = jnp.zeros_like(c_scr)
        n_scr[...] = jnp.zeros_like(n_scr)
        m_scr[...] = jnp.zeros_like(m_scr)

    u = qk_ref[...]
    ext_scr[HALO:HALO + L, :] = u
    y = cb_ref[...] + cw_ref[CONV_WIDTH - 1:CONV_WIDTH, :] * u
    for d in range(1, CONV_WIDTH):
        y = y + cw_ref[CONV_WIDTH - 1 - d:CONV_WIDTH - d, :] * ext_scr[HALO - d:HALO - d + L, :]
    ext_scr[0:HALO, :] = u[L - HALO:L, :]
    qk = y * _sigmoid(y)

    gates = gt_ref[...]
    tril = _tril(L)
    bcum = _cumsum_rows(_log_sigmoid(gates), tril.astype(BF16))
    gates_t = gates.T
    bcum_t = bcum.T

    for h in range(ML_HEADS):
        cols = slice(h * dh, (h + 1) * dh)
        q = qk[:, cols]
        k = qk[:, D_MODEL + h * dh:D_MODEL + (h + 1) * dh] * (dh ** -0.5)
        v = v_ref[:, cols]
        li_col = gates[:, ML_I0 + h:ML_I0 + h + 1]
        li_row = gates_t[ML_I0 + h:ML_I0 + h + 1, :]
        b_col = bcum[:, ML_F0 + h:ML_F0 + h + 1]
        b_row = bcum_t[ML_F0 + h:ML_F0 + h + 1, :]
        m_prev = m_scr[h][:, 0:1]
        c_t = c_scr[h]
        n_row = n_scr[h]

        dmat = jnp.where(tril, b_col - b_row + li_row, -jnp.inf)
        inter = b_col + m_prev
        m_t = jnp.maximum(jnp.max(dmat, axis=1, keepdims=True), inter)
        w_intra = jnp.exp(dmat - m_t)
        w_inter = jnp.exp(inter - m_t)

        q16 = q.astype(BF16)
        k16 = k.astype(BF16)
        s = _dot_nt(q16, k16) * w_intra
        num = _dot(s.astype(BF16), v) + w_inter * _dot(q16, c_t.astype(BF16))
        den = jnp.sum(s, axis=1, keepdims=True) + w_inter * jnp.sum(q * n_row, axis=1, keepdims=True)
        hh = num / jnp.maximum(jnp.abs(den), jnp.exp(-m_t))

        b_last = b_col[L - 1:L, :]
        log_w = b_last - b_col + li_col
        m_new = jnp.maximum(b_last + m_prev, jnp.max(log_w, axis=0, keepdims=True))
        decay = jnp.exp(b_last + m_prev - m_new)
        w = jnp.exp(log_w - m_new)
        vw = (v.astype(F32) * w).astype(BF16)
        c_scr[h] = decay * c_t + _dot_tn(k16, vw)
        n_scr[h] = decay * n_row + jnp.sum(w * k, axis=0, keepdims=True)
        m_scr[h] = jnp.broadcast_to(m_new, (1, LANES))

        mu = jnp.mean(hh, axis=1, keepdims=True)
        dlt = hh - mu
        var = jnp.mean(dlt * dlt, axis=1, keepdims=True)
        hn = dlt * lax.rsqrt(var + EPS) * ln_ref[:, cols]
        z_ref[:, cols] = _sigmoid(ga_ref[:, cols]) * (_sigmoid(o_ref[:, cols]) * hn)


def _mlstm_call(p32, p16, gates, conv_w, conv_b, ln_g, batch, seq):
    n = batch * seq
    nc = seq // ML_CHUNK
    L = ML_CHUNK

    def row(width, col):
        return pl.BlockSpec((L, width), lambda b, c: (b * nc + c, col))

    def full(shape):
        return pl.BlockSpec(shape, lambda b, c: (0, 0))

    return pl.pallas_call(
        _mlstm_kernel,
        grid=(batch, nc),
        in_specs=[
            row(2 * D_MODEL, 0),
            row(D_MODEL, 2),
            row(D_MODEL, 3),
            row(D_MODEL, 0),
            row(GATE_COLS, 0),
            full((CONV_WIDTH, 2 * D_MODEL)),
            full((1, 2 * D_MODEL)),
            full((1, D_MODEL)),
        ],
        out_specs=row(D_MODEL, 0),
        out_shape=jax.ShapeDtypeStruct((n, D_MODEL), F32),
        scratch_shapes=[
            pltpu.VMEM((HALO + L, 2 * D_MODEL), F32),
            pltpu.VMEM((ML_HEADS, ML_HEAD_DIM, ML_HEAD_DIM), F32),
            pltpu.VMEM((ML_HEADS, 1, ML_HEAD_DIM), F32),
            pltpu.VMEM((ML_HEADS, 1, LANES), F32),
        ],
        compiler_params=pltpu.CompilerParams(
            dimension_semantics=("parallel", "arbitrary"), vmem_limit_bytes=VMEM_LIMIT),
        name="mlstm",
    )(p32, p32, p32, p16, gates, conv_w, conv_b, ln_g)


def _fox_bias_features(gt_ref, qf_scr, ka_scr, seq):
    tril = _tril(CS_BLK).astype(BF16)
    r = lax.broadcasted_iota(jnp.int32, (GATE_COLS, 2 * LANES), 0)
    c = lax.broadcasted_iota(jnp.int32, (GATE_COLS, 2 * LANES), 1)
    head = r - FX_F0
    is_head = (head >= 0) & (head < FOX_HEADS)
    sel = []
    for piece in range(3):
        q_hit = is_head & (c == FOX_FEAT * head + piece)
        k_hit = is_head & (c == LANES + FOX_FEAT * head + 3 + piece)
        sel.append(jnp.where(q_hit, 1.0, jnp.where(k_hit, -1.0, 0.0)).astype(BF16))
    lane = lax.broadcasted_iota(jnp.int32, (1, 2 * LANES), 1)
    slot = lane % FOX_FEAT
    in_use = (lane % LANES) < FOX_FEAT * FOX_HEADS
    q_one = (lane < LANES) & (slot >= 3) & (slot < 6)
    k_one = (lane >= LANES) & (slot < 3)
    ones = jnp.where(in_use & (q_one | k_one), 1.0, 0.0)

    carry = jnp.zeros((1, GATE_COLS), F32)
    for blk in range(seq // CS_BLK):
        rows = slice(blk * CS_BLK, (blk + 1) * CS_BLK)
        cs = _cumsum_rows(_log_sigmoid(gt_ref[rows, :]), tril) + carry
        carry = cs[CS_BLK - 1:CS_BLK, :]
        hi = cs.astype(BF16)
        r1 = cs - hi.astype(F32)
        mid = r1.astype(BF16)
        lo = (r1 - mid.astype(F32)).astype(BF16)
        feat = (_dot(hi, sel[0]) + _dot(mid, sel[1]) + _dot(lo, sel[2]) + ones).astype(BF16)
        qf_scr[rows, :] = feat[:, 0:LANES]
        ka_scr[rows, LANES:2 * LANES] = feat[:, LANES:2 * LANES]


def _fox_kernel(q_ref, k_ref, v_ref, gb_ref, gt_ref, z_ref, qf_scr, ka_scr, qt_scr, vt_scr, tri_scr, *, seq):
    tq = FOX_TQ
    hd = FOX_HEAD_DIM
    pair = pl.program_id(1)

    @pl.when(pair == 0)
    def _():
        _fox_bias_features(gt_ref, qf_scr, ka_scr, seq)
        key = lax.broadcasted_iota(jnp.int32, (tq, tq), 0)
        qry = lax.broadcasted_iota(jnp.int32, (tq, tq), 1)
        tri_scr[...] = jnp.where(qry >= key, 0.0, -jnp.inf)
        one_row = lax.broadcasted_iota(jnp.int32, (FOX_VROWS - hd, seq), 0) == 0
        for e in range(2):
            vt_scr[e, hd:FOX_VROWS, :] = jnp.where(one_row, 1.0, 0.0).astype(BF16)

    ka_scr[:, 0:LANES] = k_ref[...]
    sub = lax.broadcasted_iota(jnp.int32, (2 * LANES, 1), 0)
    keep = []
    for e in range(2):
        feat0 = LANES + FOX_FEAT * (2 * pair + e)
        own_q = (sub >= e * FOX_HEAD_DIM) & (sub < (e + 1) * FOX_HEAD_DIM)
        own_feat = (sub >= feat0) & (sub < feat0 + FOX_FEAT)
        keep.append(jnp.where(own_q | own_feat, 1.0, 0.0))
    for blk in range(seq // tq):
        rows = slice(blk * tq, (blk + 1) * tq)
        q_aug = jnp.concatenate([q_ref[rows, :].astype(F32) * FOX_HEAD_DIM ** -0.5,
                                 qf_scr[rows, :].astype(F32)], axis=1)
        q_aug_t = q_aug.T
        v_t = v_ref[rows, :].astype(F32).T.astype(BF16)
        for e in range(2):
            qt_scr[e, :, rows] = (q_aug_t * keep[e]).astype(BF16)
            vt_scr[e, 0:hd, rows] = v_t[e * hd:(e + 1) * hd, :]

    items = []
    for qi in range(seq // tq):
        q0 = qi * tq
        spans = [(k0, 2 * tq, False) for k0 in range(0, q0 - 2 * tq + 1, 2 * tq)]
        k_done = len(spans) * 2 * tq
        spans.append((k_done, q0 + tq - k_done, True))
        for si, (k0, width, masked) in enumerate(spans):
            for e in range(2):
                items.append((q0, k0, width, masked, e, si == 0, si == len(spans) - 1))

    def logits_of(item):
        q0, k0, width, _, e, _, _ = item
        return _dot(ka_scr[k0:k0 + width, :], qt_scr[e, :, q0:q0 + tq])

    m, acc = {}, {}
    pending = logits_of(items[0])
    for idx, (q0, k0, width, masked, e, first, last) in enumerate(items):
        logit_t = pending
        if idx + 1 < len(items):
            pending = logits_of(items[idx + 1])
        if first:
            m[e] = jnp.full((1, tq), -jnp.inf, F32)
            acc[e] = jnp.zeros((FOX_VROWS, tq), F32)
        if masked:
            diag = logit_t[width - tq:] + tri_scr[...]
            logit_t = diag if width == tq else jnp.concatenate([logit_t[:width - tq], diag], axis=0)
        m_new = jnp.maximum(m[e], jnp.max(logit_t, axis=0, keepdims=True))
        alpha = jnp.exp(m[e] - m_new)
        p_t = jnp.exp(logit_t - m_new).astype(BF16)
        m[e] = m_new
        acc[e] = alpha * acc[e] + _dot(vt_scr[e, :, k0:k0 + width], p_t)
        if last and e == 1:
            o_t = jnp.concatenate([acc[h][0:hd] / acc[h][hd:hd + 1] for h in range(2)], axis=0)
            z_ref[q0:q0 + tq, :] = _sigmoid(gb_ref[q0:q0 + tq, :]) * o_t.T


def _fox_call(p32, p16, gates, batch, seq):
    n = batch * seq
    pairs = FOX_HEADS // 2
    per_group = D_MODEL // LANES

    def cols(group):
        return pl.BlockSpec((seq, LANES), lambda b, p: (b, group * per_group + p))

    return pl.pallas_call(
        functools.partial(_fox_kernel, seq=seq),
        grid=(batch, pairs),
        in_specs=[
            cols(1),
            cols(2),
            cols(3),
            cols(4),
            pl.BlockSpec((seq, GATE_COLS), lambda b, p: (b, 0)),
        ],
        out_specs=pl.BlockSpec((seq, LANES), lambda b, p: (b, p)),
        out_shape=jax.ShapeDtypeStruct((n, D_MODEL), F32),
        scratch_shapes=[
            pltpu.VMEM((seq, LANES), BF16),
            pltpu.VMEM((seq, 2 * LANES), BF16),
            pltpu.VMEM((2, 2 * LANES, seq), BF16),
            pltpu.VMEM((2, FOX_VROWS, seq), BF16),
            pltpu.VMEM((FOX_TQ, FOX_TQ), F32),
        ],
        compiler_params=pltpu.CompilerParams(
            dimension_semantics=("parallel", "arbitrary"), vmem_limit_bytes=VMEM_LIMIT),
        name="fox",
    )(p16, p16, p16, p32, gates)


def _layer(x2d, batch, seq, ffn1_norm, ffn1_w_gate, ffn1_w_up, ffn1_w_down, mix_norm, w_in, b_in,
           conv_w, conv_b, ml_head_norm, w_out, ffn2_norm, ffn2_w_gate, ffn2_w_up, ffn2_w_down, final_g):
    d = D_MODEL
    x1 = _ffn_call(x2d, ffn1_norm[None, :], ffn1_w_gate.astype(BF16), ffn1_w_up.astype(BF16),
                   ffn1_w_down.astype(BF16))

    sizes = (d, d, d, d, ML_HEADS, ML_HEADS, d, d, d, FOX_HEADS, d, d)
    starts = [0]
    for s in sizes[:-1]:
        starts.append(starts[-1] + s)
    (ml_q, ml_k, ml_v, ml_o, ml_i, ml_f, fx_q, fx_k, fx_v, fx_f, g_a, g_b) = [
        slice(a, a + s) for a, s in zip(starts, sizes)]
    order32 = (ml_q, ml_k, ml_o, g_a, g_b)
    order16 = (ml_v, fx_q, fx_k, fx_v)
    w_main = jnp.concatenate([w_in[:, c] for c in order32 + order16], axis=1).astype(BF16)
    b_main = jnp.concatenate([b_in[c] for c in order32 + order16])[None, :]
    pad = GATE_COLS - 2 * ML_HEADS - FOX_HEADS
    w_gates = jnp.concatenate([w_in[:, ml_i], w_in[:, ml_f], w_in[:, fx_f],
                               jnp.zeros((d, pad), w_in.dtype)], axis=1).astype(BF16)
    b_gates = jnp.concatenate([b_in[ml_i], b_in[ml_f], b_in[fx_f], jnp.zeros((pad,), b_in.dtype)])[None, :]
    p32, p16, gates = _proj_call(x1, mix_norm[None, :], w_main, b_main, w_gates, b_gates,
                                 len(order32), len(order16))

    z_a = _mlstm_call(p32, p16, gates, conv_w, conv_b[None, :], ml_head_norm[None, :], batch, seq)
    z_b = _fox_call(p32, p16, gates, batch, seq)

    return _ffn_call(x1, ffn2_norm[None, :], ffn2_w_gate.astype(BF16), ffn2_w_up.astype(BF16),
                     ffn2_w_down.astype(BF16), pre=(z_a, z_b, w_out.astype(BF16)), final_g=final_g)


def kernel(x, ffn1_norm, ffn1_w_gate, ffn1_w_up, ffn1_w_down, mix_norm, w_in, b_in, conv_w, conv_b,
           ml_head_norm, w_out, ffn2_norm, ffn2_w_gate, ffn2_w_up, ffn2_w_down, final_norm):
    batch, seq, d = x.shape
    assert d == D_MODEL and ffn1_norm.shape[0] == 1, "single-layer configuration"
    out = _layer(x.reshape(batch * seq, d), batch, seq, ffn1_norm[0], ffn1_w_gate[0], ffn1_w_up[0],
                 ffn1_w_down[0], mix_norm[0], w_in[0], b_in[0], conv_w[0], conv_b[0], ml_head_norm[0],
                 w_out[0], ffn2_norm[0], ffn2_w_gate[0], ffn2_w_up[0], ffn2_w_down[0], final_norm[None, :])
    return out.reshape(batch, seq, d)
```

```python
import functools

import jax
import jax.numpy as jnp
from jax import lax
from jax.experimental import pallas as pl
from jax.experimental.pallas import tpu as pltpu

D_MODEL = 1024
ML_HEADS = 4
ML_HEAD_DIM = 256
CONV_WIDTH = 4
FOX_HEADS = 16
FOX_HEAD_DIM = 64
D_FF = 2816
EPS = 1e-6

LANES = 128
GATE_COLS = LANES
ML_I0, ML_F0, FX_F0 = 0, ML_HEADS, 2 * ML_HEADS

(COL_ML_Q, COL_ML_K, COL_ML_V, COL_ML_O, COL_G_A, COL_G_B, COL_FX_Q, COL_FX_K, COL_FX_V) = range(9)
N_COL_GROUPS = 9
LOG2E = 1.4426950408889634

FFN_TM = 512
FFN_TF = 256
PROJ_TM = 1024
PROJ_TN = 2304
ML_CHUNK = 256
HALO = 8
FOX_TQ = 256
FOX_FEAT = 8
FOX_VROWS = 80
CS_BLK = 256
VMEM_LIMIT = 56 * 1024 * 1024

F32 = jnp.float32
BF16 = jnp.bfloat16


def _dot(a, b):
    return jnp.dot(a, b, preferred_element_type=F32)


def _dot_nt(a, b):
    return lax.dot_general(a, b, (((1,), (1,)), ((), ())), preferred_element_type=F32)


def _dot_tn(a, b):
    return lax.dot_general(a, b, (((0,), (0,)), ((), ())), preferred_element_type=F32)


def _rmsnorm(x, g):
    return x * lax.rsqrt(jnp.mean(x * x, axis=-1, keepdims=True) + EPS) * g


def _sigmoid(x):
    return 1.0 / (1.0 + jnp.exp(-x))


def _log_sigmoid(x):
    return jnp.minimum(x, 0.0) - jnp.log(1.0 + jnp.exp(-jnp.abs(x)))


def _cumsum_rows(x, tril):
    hi = x.astype(BF16)
    r1 = x - hi.astype(F32)
    mid = r1.astype(BF16)
    lo = (r1 - mid.astype(F32)).astype(BF16)
    return _dot(tril, hi) + _dot(tril, mid) + _dot(tril, lo)


def _tril(n):
    r = lax.broadcasted_iota(jnp.int32, (n, n), 0)
    c = lax.broadcasted_iota(jnp.int32, (n, n), 1)
    return r >= c


def _ffn_kernel(*refs, pre_proj, final_norm, next_norm):
    refs = list(refs)
    x_ref = refs.pop(0)
    if pre_proj:
        za_ref, zb_ref, wo_ref = refs.pop(0), refs.pop(0), refs.pop(0)
    g_ref, wg_ref, wu_ref, wd_ref = refs.pop(0), refs.pop(0), refs.pop(0), refs.pop(0)
    if final_norm:
        gf_ref = refs.pop(0)
    if next_norm:
        gn_ref = refs.pop(0)
    o_ref = refs.pop(0)
    if next_norm:
        hn_ref = refs.pop(0)
    (a_scr,) = refs

    x = x_ref[...]
    if pre_proj:
        y = (za_ref[...].astype(F32) + zb_ref[...].astype(F32)).astype(BF16)
        x = x + _dot(y, wo_ref[...])
    h = _rmsnorm(x, g_ref[...]).astype(BF16)
    for j in range(D_FF // FFN_TF):
        cols = slice(j * FFN_TF, (j + 1) * FFN_TF)
        gate = _dot(h, wg_ref[:, cols])
        up = _dot(h, wu_ref[:, cols])
        a_scr[:, cols] = (gate * _sigmoid(gate) * up).astype(BF16)
    out = x + 0.5 * _dot(a_scr[...], wd_ref[...])
    if next_norm:
        hn_ref[...] = _rmsnorm(out, gn_ref[...]).astype(BF16)
    if final_norm:
        out = _rmsnorm(out, gf_ref[...])
    o_ref[...] = out


def _ffn_call(x, norm_g, wg, wu, wd, pre=None, final_g=None, next_g=None):
    n = x.shape[0]
    row = pl.BlockSpec((FFN_TM, D_MODEL), lambda i: (i, 0))
    vec = pl.BlockSpec((1, D_MODEL), lambda i: (0, 0))
    out_specs, out_shape = row, jax.ShapeDtypeStruct((n, D_MODEL), F32)
    if next_g is not None:
        out_specs, out_shape = [row, row], [out_shape, jax.ShapeDtypeStruct((n, D_MODEL), BF16)]

    def resident(shape):
        return pl.BlockSpec(shape, lambda i: (0, 0), pipeline_mode=pl.Buffered(1))

    args, specs = [x], [row]
    if pre is not None:
        za, zb, wo = pre
        args += [za, zb, wo]
        specs += [row, row, resident((D_MODEL, D_MODEL))]
    args += [norm_g, wg, wu, wd]
    specs += [vec, resident((D_MODEL, D_FF)), resident((D_MODEL, D_FF)), resident((D_FF, D_MODEL))]
    for g in (final_g, next_g):
        if g is not None:
            args.append(g)
            specs.append(vec)
    return pl.pallas_call(
        functools.partial(_ffn_kernel, pre_proj=pre is not None, final_norm=final_g is not None,
                          next_norm=next_g is not None),
        grid=(n // FFN_TM,),
        in_specs=specs,
        out_specs=out_specs,
        out_shape=out_shape,
        scratch_shapes=[pltpu.VMEM((FFN_TM, D_FF), BF16)],
        compiler_params=pltpu.CompilerParams(
            dimension_semantics=("parallel",), vmem_limit_bytes=VMEM_LIMIT),
        name="ffn2" if pre is not None else "ffn1",
    )(*args)


def _proj_kernel(h_ref, w_ref, b_ref, s_ref, wgt_ref, bgt_ref, p_ref, gt_ref):
    h = h_ref[...]

    @pl.when(pl.program_id(1) == 0)
    def _():
        gt_ref[...] = _dot(h, wgt_ref[...]) + bgt_ref[...]

    p_ref[...] = ((_dot(h, w_ref[...]) + b_ref[...]) * s_ref[...]).astype(BF16)


def _proj_call(h, w, b, col_scale, w_gates, b_gates):
    n = h.shape[0]
    n_out = w.shape[1]
    return pl.pallas_call(
        _proj_kernel,
        grid=(n // PROJ_TM, n_out // PROJ_TN),
        in_specs=[
            pl.BlockSpec((PROJ_TM, D_MODEL), lambda i, j: (i, 0)),
            pl.BlockSpec((D_MODEL, PROJ_TN), lambda i, j: (0, j)),
            pl.BlockSpec((1, PROJ_TN), lambda i, j: (0, j)),
            pl.BlockSpec((1, PROJ_TN), lambda i, j: (0, j)),
            pl.BlockSpec((D_MODEL, GATE_COLS), lambda i, j: (0, 0)),
            pl.BlockSpec((1, GATE_COLS), lambda i, j: (0, 0)),
        ],
        out_specs=[
            pl.BlockSpec((PROJ_TM, PROJ_TN), lambda i, j: (i, j)),
            pl.BlockSpec((PROJ_TM, GATE_COLS), lambda i, j: (i, 0)),
        ],
        out_shape=[
            jax.ShapeDtypeStruct((n, n_out), BF16),
            jax.ShapeDtypeStruct((n, GATE_COLS), F32),
        ],
        compiler_params=pltpu.CompilerParams(
            dimension_semantics=("parallel", "arbitrary"), vmem_limit_bytes=VMEM_LIMIT),
        name="in_proj",
    )(h, w, b, col_scale, w_gates, b_gates)


def _mlstm_kernel(qk_ref, o_ref, ga_ref, v_ref, gt_ref, cw_ref, cb_ref, ln_ref, z_ref,
                  ext_scr, c_scr, n_scr, m_scr):
    L = ML_CHUNK
    dh = ML_HEAD_DIM

    @pl.when(pl.program_id(1) == 0)
    def _():
        ext_scr[0:HALO, :] = jnp.zeros((HALO, 2 * D_MODEL), F32)
        c_scr[...] = jnp.zeros_like(c_scr)
        n_scr[...] = jnp.zeros_like(n_scr)
        m_scr[...] = jnp.zeros_like(m_scr)

    u = qk_ref[...].astype(F32)
    ext_scr[HALO:HALO + L, :] = u
    y = cb_ref[...] + cw_ref[CONV_WIDTH - 1:CONV_WIDTH, :] * u
    for d in range(1, CONV_WIDTH):
        y = y + cw_ref[CONV_WIDTH - 1 - d:CONV_WIDTH - d, :] * ext_scr[HALO - d:HALO - d + L, :]
    ext_scr[0:HALO, :] = u[L - HALO:L, :]
    qk = y * _sigmoid(y)

    gates = gt_ref[...]
    tril = _tril(L)
    bcum = _cumsum_rows(_log_sigmoid(gates), tril.astype(BF16))
    gates_t = gates.T
    bcum_t = bcum.T

    for h in range(ML_HEADS):
        cols = slice(h * dh, (h + 1) * dh)
        q = qk[:, cols]
        k = qk[:, D_MODEL + h * dh:D_MODEL + (h + 1) * dh] * (dh ** -0.5)
        v = v_ref[:, cols]
        li_col = gates[:, ML_I0 + h:ML_I0 + h + 1]
        li_row = gates_t[ML_I0 + h:ML_I0 + h + 1, :]
        b_col = bcum[:, ML_F0 + h:ML_F0 + h + 1]
        b_row = bcum_t[ML_F0 + h:ML_F0 + h + 1, :]
        m_prev = m_scr[h][:, 0:1]
        c_t = c_scr[h]
        n_row = n_scr[h]

        dmat = jnp.where(tril, b_col - b_row + li_row, -jnp.inf)
        inter = b_col + m_prev
        m_t = jnp.maximum(jnp.max(dmat, axis=1, keepdims=True), inter)
        w_intra = jnp.exp(dmat - m_t)
        w_inter = jnp.exp(inter - m_t)

        q16 = q.astype(BF16)
        k16 = k.astype(BF16)
        s = _dot_nt(q16, k16) * w_intra
        num = _dot(s.astype(BF16), v) + w_inter * _dot(q16, c_t.astype(BF16))
        den = jnp.sum(s, axis=1, keepdims=True) + w_inter * jnp.sum(q * n_row, axis=1, keepdims=True)
        hh = num / jnp.maximum(jnp.abs(den), jnp.exp(-m_t))

        b_last = b_col[L - 1:L, :]
        log_w = b_last - b_col + li_col
        m_new = jnp.maximum(b_last + m_prev, jnp.max(log_w, axis=0, keepdims=True))
        decay = jnp.exp(b_last + m_prev - m_new)
        w = jnp.exp(log_w - m_new)
        vw = (v.astype(F32) * w).astype(BF16)
        c_scr[h] = decay * c_t + _dot_tn(k16, vw)
        n_scr[h] = decay * n_row + jnp.sum(w * k, axis=0, keepdims=True)
        m_scr[h] = jnp.broadcast_to(m_new, (1, LANES))

        mu = jnp.mean(hh, axis=1, keepdims=True)
        dlt = hh - mu
        var = jnp.mean(dlt * dlt, axis=1, keepdims=True)
        hn = dlt * lax.rsqrt(var + EPS) * ln_ref[:, cols]
        gate = _sigmoid(ga_ref[:, cols].astype(F32)) * _sigmoid(o_ref[:, cols].astype(F32))
        z_ref[:, cols] = (gate * hn).astype(BF16)


def _mlstm_call(proj, gates, conv_w, conv_b, ln_g, batch, seq):
    n = batch * seq
    nc = seq // ML_CHUNK
    L = ML_CHUNK

    def row(width, col):
        return pl.BlockSpec((L, width), lambda b, c: (b * nc + c, col))

    def full(shape):
        return pl.BlockSpec(shape, lambda b, c: (0, 0))

    return pl.pallas_call(
        _mlstm_kernel,
        grid=(batch, nc),
        in_specs=[
            row(2 * D_MODEL, COL_ML_Q // 2),
            row(D_MODEL, COL_ML_O),
            row(D_MODEL, COL_G_A),
            row(D_MODEL, COL_ML_V),
            row(GATE_COLS, 0),
            full((CONV_WIDTH, 2 * D_MODEL)),
            full((1, 2 * D_MODEL)),
            full((1, D_MODEL)),
        ],
        out_specs=row(D_MODEL, 0),
        out_shape=jax.ShapeDtypeStruct((n, D_MODEL), BF16),
        scratch_shapes=[
            pltpu.VMEM((HALO + L, 2 * D_MODEL), F32),
            pltpu.VMEM((ML_HEADS, ML_HEAD_DIM, ML_HEAD_DIM), F32),
            pltpu.VMEM((ML_HEADS, 1, ML_HEAD_DIM), F32),
            pltpu.VMEM((ML_HEADS, 1, LANES), F32),
        ],
        compiler_params=pltpu.CompilerParams(
            dimension_semantics=("parallel", "arbitrary"), vmem_limit_bytes=VMEM_LIMIT),
        name="mlstm",
    )(proj, proj, proj, proj, gates, conv_w, conv_b, ln_g)


def _fox_bias_features(gt_ref, qf_scr, ka_scr, seq):
    tril = _tril(CS_BLK).astype(BF16)
    r = lax.broadcasted_iota(jnp.int32, (GATE_COLS, 2 * LANES), 0)
    c = lax.broadcasted_iota(jnp.int32, (GATE_COLS, 2 * LANES), 1)
    head = r - FX_F0
    is_head = (head >= 0) & (head < FOX_HEADS)
    sel = []
    for piece in range(3):
        q_hit = is_head & (c == FOX_FEAT * head + piece)
        k_hit = is_head & (c == LANES + FOX_FEAT * head + 3 + piece)
        sel.append(jnp.where(q_hit, 1.0, jnp.where(k_hit, -1.0, 0.0)).astype(BF16))
    lane = lax.broadcasted_iota(jnp.int32, (1, 2 * LANES), 1)
    slot = lane % FOX_FEAT
    in_use = (lane % LANES) < FOX_FEAT * FOX_HEADS
    q_one = (lane < LANES) & (slot >= 3) & (slot < 6)
    k_one = (lane >= LANES) & (slot < 3)
    ones = jnp.where(in_use & (q_one | k_one), 1.0, 0.0)

    carry = jnp.zeros((1, GATE_COLS), F32)
    for blk in range(seq // CS_BLK):
        rows = slice(blk * CS_BLK, (blk + 1) * CS_BLK)
        cs = _cumsum_rows(_log_sigmoid(gt_ref[rows, :]), tril) + carry
        carry = cs[CS_BLK - 1:CS_BLK, :]
        cs2 = cs * LOG2E
        hi = cs2.astype(BF16)
        r1 = cs2 - hi.astype(F32)
        mid = r1.astype(BF16)
        lo = (r1 - mid.astype(F32)).astype(BF16)
        feat = (_dot(hi, sel[0]) + _dot(mid, sel[1]) + _dot(lo, sel[2]) + ones).astype(BF16)
        qf_scr[rows, :] = feat[:, 0:LANES]
        ka_scr[rows, LANES:2 * LANES] = feat[:, LANES:2 * LANES]


def _fox_kernel(q_ref, k_ref, v_ref, gb_ref, gt_ref, z_ref, qf_scr, ka_scr, qt_scr, vt_scr, tri_scr, *, seq):
    tq = FOX_TQ
    hd = FOX_HEAD_DIM
    pair = pl.program_id(1)

    @pl.when(pair == 0)
    def _():
        _fox_bias_features(gt_ref, qf_scr, ka_scr, seq)
        key = lax.broadcasted_iota(jnp.int32, (tq, tq), 0)
        qry = lax.broadcasted_iota(jnp.int32, (tq, tq), 1)
        tri_scr[...] = jnp.where(qry >= key, 0.0, -jnp.inf)
        one_row = lax.broadcasted_iota(jnp.int32, (FOX_VROWS - hd, seq), 0) == 0
        for e in range(2):
            vt_scr[e, hd:FOX_VROWS, :] = jnp.where(one_row, 1.0, 0.0).astype(BF16)

    ka_scr[:, 0:LANES] = k_ref[...]
    sub = lax.broadcasted_iota(jnp.int32, (2 * LANES, 1), 0)
    keep = []
    for e in range(2):
        feat0 = LANES + FOX_FEAT * (2 * pair + e)
        own_q = (sub >= e * FOX_HEAD_DIM) & (sub < (e + 1) * FOX_HEAD_DIM)
        own_feat = (sub >= feat0) & (sub < feat0 + FOX_FEAT)
        keep.append(jnp.where(own_q | own_feat, 1.0, 0.0))
    for blk in range(seq // tq):
        rows = slice(blk * tq, (blk + 1) * tq)
        q_aug = jnp.concatenate([q_ref[rows, :].astype(F32), qf_scr[rows, :].astype(F32)], axis=1)
        q_aug_t = q_aug.T
        v_t = v_ref[rows, :].astype(F32).T.astype(BF16)
        for e in range(2):
            qt_scr[e, :, rows] = (q_aug_t * keep[e]).astype(BF16)
            vt_scr[e, 0:hd, rows] = v_t[e * hd:(e + 1) * hd, :]

    items = []
    for qi in range(seq // tq):
        q0 = qi * tq
        spans = [(k0, 2 * tq, False) for k0 in range(0, q0 - 2 * tq + 1, 2 * tq)]
        k_done = len(spans) * 2 * tq
        spans.append((k_done, q0 + tq - k_done, True))
        for si, (k0, width, masked) in enumerate(spans):
            for e in range(2):
                items.append((q0, k0, width, masked, e, si == 0, si == len(spans) - 1))

    def logits_of(item):
        q0, k0, width, _, e, _, _ = item
        return _dot(ka_scr[k0:k0 + width, :], qt_scr[e, :, q0:q0 + tq])

    m, acc = {}, {}
    pending = logits_of(items[0])
    for idx, (q0, k0, width, masked, e, first, last) in enumerate(items):
        logit_t = pending
        if idx + 1 < len(items):
            pending = logits_of(items[idx + 1])
        if first:
            m[e] = jnp.full((1, tq), -jnp.inf, F32)
            acc[e] = jnp.zeros((FOX_VROWS, tq), F32)
        if masked:
            diag = logit_t[width - tq:] + tri_scr[...]
            logit_t = diag if width == tq else jnp.concatenate([logit_t[:width - tq], diag], axis=0)
        m_new = jnp.maximum(m[e], jnp.max(logit_t, axis=0, keepdims=True))
        alpha = jnp.exp2(m[e] - m_new)
        p_t = jnp.exp2(logit_t - m_new).astype(BF16)
        m[e] = m_new
        acc[e] = alpha * acc[e] + _dot(vt_scr[e, :, k0:k0 + width], p_t)
        if last and e == 1:
            o_t = jnp.concatenate([acc[h][0:hd] / acc[h][hd:hd + 1] for h in range(2)], axis=0)
            z_ref[q0:q0 + tq, :] = (_sigmoid(gb_ref[q0:q0 + tq, :].astype(F32)) * o_t.T).astype(BF16)


def _fox_call(proj, gates, batch, seq):
    n = batch * seq
    pairs = FOX_HEADS // 2
    per_group = D_MODEL // LANES

    def cols(group):
        return pl.BlockSpec((seq, LANES), lambda b, p: (b, group * per_group + p))

    return pl.pallas_call(
        functools.partial(_fox_kernel, seq=seq),
        grid=(batch, pairs),
        in_specs=[
            cols(COL_FX_Q),
            cols(COL_FX_K),
            cols(COL_FX_V),
            cols(COL_G_B),
            pl.BlockSpec((seq, GATE_COLS), lambda b, p: (b, 0)),
        ],
        out_specs=pl.BlockSpec((seq, LANES), lambda b, p: (b, p)),
        out_shape=jax.ShapeDtypeStruct((n, D_MODEL), BF16),
        scratch_shapes=[
            pltpu.VMEM((seq, LANES), BF16),
            pltpu.VMEM((seq, 2 * LANES), BF16),
            pltpu.VMEM((2, 2 * LANES, seq), BF16),
            pltpu.VMEM((2, FOX_VROWS, seq), BF16),
            pltpu.VMEM((FOX_TQ, FOX_TQ), F32),
        ],
        compiler_params=pltpu.CompilerParams(
            dimension_semantics=("parallel", "arbitrary"), vmem_limit_bytes=VMEM_LIMIT),
        name="fox",
    )(proj, proj, proj, proj, gates)


def _layer(x2d, batch, seq, ffn1_norm, ffn1_w_gate, ffn1_w_up, ffn1_w_down, mix_norm, w_in, b_in,
           conv_w, conv_b, ml_head_norm, w_out, ffn2_norm, ffn2_w_gate, ffn2_w_up, ffn2_w_down, final_g):
    d = D_MODEL
    x1, h1 = _ffn_call(x2d, ffn1_norm[None, :], ffn1_w_gate.astype(BF16), ffn1_w_up.astype(BF16),
                       ffn1_w_down.astype(BF16), next_g=mix_norm[None, :])

    sizes = (d, d, d, d, ML_HEADS, ML_HEADS, d, d, d, FOX_HEADS, d, d)
    starts = [0]
    for s in sizes[:-1]:
        starts.append(starts[-1] + s)
    (ml_q, ml_k, ml_v, ml_o, ml_i, ml_f, fx_q, fx_k, fx_v, fx_f, g_a, g_b) = [
        slice(a, a + s) for a, s in zip(starts, sizes)]
    order = [None] * N_COL_GROUPS
    for col, grp in ((COL_ML_Q, ml_q), (COL_ML_K, ml_k), (COL_ML_V, ml_v), (COL_ML_O, ml_o), (COL_G_A, g_a),
                     (COL_G_B, g_b), (COL_FX_Q, fx_q), (COL_FX_K, fx_k), (COL_FX_V, fx_v)):
        order[col] = grp
    w_main = jnp.concatenate([w_in[:, c] for c in order], axis=1).astype(BF16)
    b_main = jnp.concatenate([b_in[c] for c in order])[None, :]
    col_scale = jnp.ones((N_COL_GROUPS, d), F32).at[COL_FX_Q].set(LOG2E * FOX_HEAD_DIM ** -0.5).reshape(1, -1)
    pad = GATE_COLS - 2 * ML_HEADS - FOX_HEADS
    w_gates = jnp.concatenate([w_in[:, ml_i], w_in[:, ml_f], w_in[:, fx_f],
                               jnp.zeros((d, pad), w_in.dtype)], axis=1).astype(BF16)
    b_gates = jnp.concatenate([b_in[ml_i], b_in[ml_f], b_in[fx_f], jnp.zeros((pad,), b_in.dtype)])[None, :]
    proj, gates = _proj_call(h1, w_main, b_main, col_scale, w_gates, b_gates)

    z_a = _mlstm_call(proj, gates, conv_w, conv_b[None, :], ml_head_norm[None, :], batch, seq)
    z_b = _fox_call(proj, gates, batch, seq)

    return _ffn_call(x1, ffn2_norm[None, :], ffn2_w_gate.astype(BF16), ffn2_w_up.astype(BF16),
                     ffn2_w_down.astype(BF16), pre=(z_a, z_b, w_out.astype(BF16)), final_g=final_g)


def kernel(x, ffn1_norm, ffn1_w_gate, ffn1_w_up, ffn1_w_down, mix_norm, w_in, b_in, conv_w, conv_b,
           ml_head_norm, w_out, ffn2_norm, ffn2_w_gate, ffn2_w_up, ffn2_w_down, final_norm):
    batch, seq, d = x.shape
    assert d == D_MODEL and ffn1_norm.shape[0] == 1, "single-layer configuration"
    out = _layer(x.reshape(batch * seq, d), batch, seq, ffn1_norm[0], ffn1_w_gate[0], ffn1_w_up[0],
                 ffn1_w_down[0], mix_norm[0], w_in[0], b_in[0], conv_w[0], conv_b[0], ml_head_norm[0],
                 w_out[0], ffn2_norm[0], ffn2_w_gate[0], ffn2_w_up[0], ffn2_w_down[0], final_norm[None, :])
    return out.reshape(batch, seq, d)
```

```python
import functools

import jax
import jax.numpy as jnp
from jax import lax
from jax.experimental import pallas as pl
from jax.experimental.pallas import tpu as pltpu

D_MODEL = 1024
ML_HEADS = 4
ML_HEAD_DIM = 256
CONV_WIDTH = 4
FOX_HEADS = 16
FOX_HEAD_DIM = 64
D_FF = 2816
EPS = 1e-6

LANES = 128
GATE_COLS = LANES
ML_I0, ML_F0, FX_F0 = 0, ML_HEADS, 2 * ML_HEADS

(COL_ML_Q, COL_ML_K, COL_ML_V, COL_ML_O, COL_G_A, COL_G_B, COL_FX_Q, COL_FX_K, COL_FX_V) = range(9)
N_COL_GROUPS = 9
LOG2E = 1.4426950408889634

FFN_TM = 512
FFN_TF = 256
PROJ_TM = 1024
PROJ_TN = 2304
ML_CHUNK = 256
HALO = 16
ML_PAD = 16
FOX_TQ = 256
FOX_TK = 2048
FOX_FEAT = 8
FOX_VROWS = 80
CS_BLK = 256
VMEM_LIMIT = 56 * 1024 * 1024

F32 = jnp.float32
BF16 = jnp.bfloat16


def _dot(a, b):
    return jnp.dot(a, b, preferred_element_type=F32)


def _dot_nt(a, b):
    return lax.dot_general(a, b, (((1,), (1,)), ((), ())), preferred_element_type=F32)


def _dot_tn(a, b):
    return lax.dot_general(a, b, (((0,), (0,)), ((), ())), preferred_element_type=F32)


def _rmsnorm(x, g):
    return x * lax.rsqrt(jnp.mean(x * x, axis=-1, keepdims=True) + EPS) * g


def _sigmoid(x):
    return 1.0 / (1.0 + jnp.exp2(x * (-LOG2E)))


def _log_sigmoid(x):
    return jnp.minimum(x, 0.0) - jnp.log(1.0 + jnp.exp(-jnp.abs(x)))


def _cumsum_rows(x, tril):
    hi = x.astype(BF16)
    r1 = x - hi.astype(F32)
    mid = r1.astype(BF16)
    lo = (r1 - mid.astype(F32)).astype(BF16)
    return _dot(tril, hi) + _dot(tril, mid) + _dot(tril, lo)


def _tril(n):
    r = lax.broadcasted_iota(jnp.int32, (n, n), 0)
    c = lax.broadcasted_iota(jnp.int32, (n, n), 1)
    return r >= c


def _ffn_kernel(*refs, pre_proj, final_norm, next_norm):
    refs = list(refs)
    x_ref = refs.pop(0)
    if pre_proj:
        za_ref, zb_ref, wo_ref = refs.pop(0), refs.pop(0), refs.pop(0)
    g_ref, wg_ref, wu_ref, wd_ref = refs.pop(0), refs.pop(0), refs.pop(0), refs.pop(0)
    if final_norm:
        gf_ref = refs.pop(0)
    if next_norm:
        gn_ref = refs.pop(0)
    o_ref = refs.pop(0)
    if next_norm:
        hn_ref = refs.pop(0)
    (a_scr,) = refs

    x = x_ref[...]
    if pre_proj:
        y = (za_ref[...].astype(F32) + zb_ref[...].astype(F32)).astype(BF16)
        x = x + _dot(y, wo_ref[...])
    h = _rmsnorm(x, g_ref[...]).astype(BF16)
    for j in range(D_FF // FFN_TF):
        cols = slice(j * FFN_TF, (j + 1) * FFN_TF)
        gate = _dot(h, wg_ref[:, cols])
        up = _dot(h, wu_ref[:, cols])
        a_scr[:, cols] = (gate * _sigmoid(gate) * up).astype(BF16)
    out = x + 0.5 * _dot(a_scr[...], wd_ref[...])
    if next_norm:
        hn_ref[...] = _rmsnorm(out, gn_ref[...]).astype(BF16)
    if final_norm:
        out = _rmsnorm(out, gf_ref[...])
    o_ref[...] = out


def _ffn_call(x, norm_g, wg, wu, wd, pre=None, final_g=None, next_g=None):
    n = x.shape[0]
    row = pl.BlockSpec((FFN_TM, D_MODEL), lambda i: (i, 0))
    vec = pl.BlockSpec((1, D_MODEL), lambda i: (0, 0))
    out_specs, out_shape = row, jax.ShapeDtypeStruct((n, D_MODEL), F32)
    if next_g is not None:
        out_specs, out_shape = [row, row], [out_shape, jax.ShapeDtypeStruct((n, D_MODEL), BF16)]

    def resident(shape):
        return pl.BlockSpec(shape, lambda i: (0, 0), pipeline_mode=pl.Buffered(1))

    args, specs = [x], [row]
    if pre is not None:
        za, zb, wo = pre
        args += [za, zb, wo]
        specs += [row, row, resident((D_MODEL, D_MODEL))]
    args += [norm_g, wg, wu, wd]
    specs += [vec, resident((D_MODEL, D_FF)), resident((D_MODEL, D_FF)), resident((D_FF, D_MODEL))]
    for g in (final_g, next_g):
        if g is not None:
            args.append(g)
            specs.append(vec)
    return pl.pallas_call(
        functools.partial(_ffn_kernel, pre_proj=pre is not None, final_norm=final_g is not None,
                          next_norm=next_g is not None),
        grid=(n // FFN_TM,),
        in_specs=specs,
        out_specs=out_specs,
        out_shape=out_shape,
        scratch_shapes=[pltpu.VMEM((FFN_TM, D_FF), BF16)],
        compiler_params=pltpu.CompilerParams(
            dimension_semantics=("parallel",), vmem_limit_bytes=VMEM_LIMIT),
        name="ffn2" if pre is not None else "ffn1",
    )(*args)


def _proj_kernel(h_ref, w_ref, b_ref, s_ref, wgt_ref, bgt_ref, p_ref, gt_ref):
    h = h_ref[...]

    @pl.when(pl.program_id(1) == 0)
    def _():
        gt_ref[...] = _dot(h, wgt_ref[...]) + bgt_ref[...]

    p_ref[...] = ((_dot(h, w_ref[...]) + b_ref[...]) * s_ref[...]).astype(BF16)


def _proj_call(h, w, b, col_scale, w_gates, b_gates):
    n = h.shape[0]
    n_out = w.shape[1]
    return pl.pallas_call(
        _proj_kernel,
        grid=(n // PROJ_TM, n_out // PROJ_TN),
        in_specs=[
            pl.BlockSpec((PROJ_TM, D_MODEL), lambda i, j: (i, 0)),
            pl.BlockSpec((D_MODEL, PROJ_TN), lambda i, j: (0, j)),
            pl.BlockSpec((1, PROJ_TN), lambda i, j: (0, j)),
            pl.BlockSpec((1, PROJ_TN), lambda i, j: (0, j)),
            pl.BlockSpec((D_MODEL, GATE_COLS), lambda i, j: (0, 0)),
            pl.BlockSpec((1, GATE_COLS), lambda i, j: (0, 0)),
        ],
        out_specs=[
            pl.BlockSpec((PROJ_TM, PROJ_TN), lambda i, j: (i, j)),
            pl.BlockSpec((PROJ_TM, GATE_COLS), lambda i, j: (i, 0)),
        ],
        out_shape=[
            jax.ShapeDtypeStruct((n, n_out), BF16),
            jax.ShapeDtypeStruct((n, GATE_COLS), F32),
        ],
        compiler_params=pltpu.CompilerParams(
            dimension_semantics=("parallel", "arbitrary"), vmem_limit_bytes=VMEM_LIMIT),
        name="in_proj",
    )(h, w, b, col_scale, w_gates, b_gates)


def _mlstm_kernel(qk_ref, o_ref, ga_ref, v_ref, gt_ref, cw_ref, cb_ref, ln_ref, z_ref,
                  halo_scr, c_scr, m_scr):
    L = ML_CHUNK
    dh = ML_HEAD_DIM
    rows = dh + ML_PAD

    @pl.when(pl.program_id(1) == 0)
    def _():
        halo_scr[...] = jnp.zeros_like(halo_scr)
        c_scr[...] = jnp.zeros_like(c_scr)
        m_scr[...] = jnp.zeros_like(m_scr)

    u16 = qk_ref[...]
    halo16 = halo_scr[...]
    t_out = lax.broadcasted_iota(jnp.int32, (L, L), 0)
    t_in = lax.broadcasted_iota(jnp.int32, (L, L), 1)
    h_out = lax.broadcasted_iota(jnp.int32, (HALO, HALO), 0)
    h_in = lax.broadcasted_iota(jnp.int32, (HALO, HALO), 1)
    y = cb_ref[...] + cw_ref[CONV_WIDTH - 1:CONV_WIDTH, :] * u16.astype(F32)
    head_fix = jnp.zeros((HALO, 2 * D_MODEL), F32)
    for d in range(1, CONV_WIDTH):
        w_d = cw_ref[CONV_WIDTH - 1 - d:CONV_WIDTH - d, :]
        y = y + w_d * _dot(jnp.where(t_in == t_out - d, 1.0, 0.0).astype(BF16), u16)
        head_fix = head_fix + w_d * _dot(jnp.where(h_in == h_out - d + HALO, 1.0, 0.0).astype(BF16), halo16)
    y = jnp.concatenate([y[0:HALO] + head_fix, y[HALO:]], axis=0)
    halo_scr[...] = u16[L - HALO:L, :]
    qk = y * _sigmoid(y)

    gates = gt_ref[...]
    tril = _tril(L)
    bcum = _cumsum_rows(_log_sigmoid(gates), tril.astype(BF16))
    gates_t = gates.T
    bcum_t = bcum.T
    pad_row = lax.broadcasted_iota(jnp.int32, (ML_PAD, L), 0)
    ones_pad = jnp.where(pad_row == 0, 1.0, 0.0).astype(BF16)

    def matmul_stage(h):
        q_t = qk[:, h * dh:(h + 1) * dh].T.astype(BF16)
        k16 = (qk[:, D_MODEL + h * dh:D_MODEL + (h + 1) * dh] * (dh ** -0.5)).astype(BF16)
        state = c_scr[h]
        return q_t, k16, state, _dot(k16, q_t), _dot(state.astype(BF16), q_t)

    staged = matmul_stage(0)
    for h in range(ML_HEADS):
        cols = slice(h * dh, (h + 1) * dh)
        q_t, k16, state, qk_t, inter_t = staged
        if h + 1 < ML_HEADS:
            staged = matmul_stage(h + 1)
        v_aug = jnp.concatenate([v_ref[:, cols].astype(F32).T.astype(BF16), ones_pad], axis=0)
        li_row = gates_t[ML_I0 + h:ML_I0 + h + 1, :]
        b_row = bcum_t[ML_F0 + h:ML_F0 + h + 1, :]
        ib_col = gates[:, ML_I0 + h:ML_I0 + h + 1] - bcum[:, ML_F0 + h:ML_F0 + h + 1]
        m_prev = m_scr[h][:, 0:1]

        dmat_t = jnp.where(t_out <= t_in, b_row + ib_col, -jnp.inf)
        inter = b_row + m_prev
        m_t = jnp.maximum(jnp.max(dmat_t, axis=0, keepdims=True), inter)
        w_inter = jnp.exp(inter - m_t)
        s_t = (qk_t * jnp.exp(dmat_t - m_t)).astype(BF16)
        num_den = _dot(v_aug, s_t) + w_inter * inter_t
        den = num_den[dh:dh + 1, :]
        hh_t = num_den[0:dh, :] / jnp.maximum(jnp.abs(den), jnp.exp(-m_t))

        b_last = b_row[:, L - 1:L]
        log_w = b_last - b_row + li_row
        m_new = jnp.maximum(b_last + m_prev, jnp.max(log_w, axis=1, keepdims=True))
        decay = jnp.exp(b_last + m_prev - m_new)
        w = jnp.exp(log_w - m_new)
        c_scr[h] = decay * state + _dot((v_aug.astype(F32) * w).astype(BF16), k16)
        m_scr[h] = jnp.broadcast_to(m_new, (1, LANES))

        mu = jnp.mean(hh_t, axis=0, keepdims=True)
        dlt = hh_t - mu
        var = jnp.mean(dlt * dlt, axis=0, keepdims=True)
        hn = (dlt * lax.rsqrt(var + EPS)).T * ln_ref[:, cols]
        gate = _sigmoid(ga_ref[:, cols].astype(F32)) * _sigmoid(o_ref[:, cols].astype(F32))
        z_ref[:, cols] = (gate * hn).astype(BF16)


def _mlstm_call(proj, gates, conv_w, conv_b, ln_g, batch, seq):
    n = batch * seq
    nc = seq // ML_CHUNK
    L = ML_CHUNK

    def row(width, col):
        return pl.BlockSpec((L, width), lambda b, c: (b * nc + c, col))

    def full(shape):
        return pl.BlockSpec(shape, lambda b, c: (0, 0))

    return pl.pallas_call(
        _mlstm_kernel,
        grid=(batch, nc),
        in_specs=[
            row(2 * D_MODEL, COL_ML_Q // 2),
            row(D_MODEL, COL_ML_O),
            row(D_MODEL, COL_G_A),
            row(D_MODEL, COL_ML_V),
            row(GATE_COLS, 0),
            full((CONV_WIDTH, 2 * D_MODEL)),
            full((1, 2 * D_MODEL)),
            full((1, D_MODEL)),
        ],
        out_specs=row(D_MODEL, 0),
        out_shape=jax.ShapeDtypeStruct((n, D_MODEL), BF16),
        scratch_shapes=[
            pltpu.VMEM((HALO, 2 * D_MODEL), BF16),
            pltpu.VMEM((ML_HEADS, ML_HEAD_DIM + ML_PAD, ML_HEAD_DIM), F32),
            pltpu.VMEM((ML_HEADS, 1, LANES), F32),
        ],
        compiler_params=pltpu.CompilerParams(
            dimension_semantics=("parallel", "arbitrary"), vmem_limit_bytes=VMEM_LIMIT),
        name="mlstm",
    )(proj, proj, proj, proj, gates, conv_w, conv_b, ln_g)


def _fox_bias_features(gt_ref, qf_scr, ka_scr, seq):
    tril = _tril(CS_BLK).astype(BF16)
    r = lax.broadcasted_iota(jnp.int32, (GATE_COLS, 2 * LANES), 0)
    c = lax.broadcasted_iota(jnp.int32, (GATE_COLS, 2 * LANES), 1)
    head = r - FX_F0
    is_head = (head >= 0) & (head < FOX_HEADS)
    sel = []
    for piece in range(3):
        q_hit = is_head & (c == FOX_FEAT * head + piece)
        k_hit = is_head & (c == LANES + FOX_FEAT * head + 3 + piece)
        sel.append(jnp.where(q_hit, 1.0, jnp.where(k_hit, -1.0, 0.0)).astype(BF16))
    lane = lax.broadcasted_iota(jnp.int32, (1, 2 * LANES), 1)
    slot = lane % FOX_FEAT
    in_use = (lane % LANES) < FOX_FEAT * FOX_HEADS
    q_one = (lane < LANES) & (slot >= 3) & (slot < 6)
    k_one = (lane >= LANES) & (slot < 3)
    ones = jnp.where(in_use & (q_one | k_one), 1.0, 0.0)

    carry = jnp.zeros((1, GATE_COLS), F32)
    for blk in range(seq // CS_BLK):
        rows = slice(blk * CS_BLK, (blk + 1) * CS_BLK)
        cs = _cumsum_rows(_log_sigmoid(gt_ref[rows, :]), tril) + carry
        carry = cs[CS_BLK - 1:CS_BLK, :]
        cs2 = cs * LOG2E
        hi = cs2.astype(BF16)
        r1 = cs2 - hi.astype(F32)
        mid = r1.astype(BF16)
        lo = (r1 - mid.astype(F32)).astype(BF16)
        feat = (_dot(hi, sel[0]) + _dot(mid, sel[1]) + _dot(lo, sel[2]) + ones).astype(BF16)
        qf_scr[rows, :] = feat[:, 0:LANES]
        ka_scr[rows, LANES:2 * LANES] = feat[:, LANES:2 * LANES]


def _fox_kernel(q_ref, k_ref, v_ref, gb_ref, gt_ref, z_ref, qf_scr, ka_scr, qt_scr, vt_scr, tri_scr, *, seq):
    tq = FOX_TQ
    hd = FOX_HEAD_DIM
    pair = pl.program_id(1)

    @pl.when(pair == 0)
    def _():
        _fox_bias_features(gt_ref, qf_scr, ka_scr, seq)
        key = lax.broadcasted_iota(jnp.int32, (tq, tq), 0)
        qry = lax.broadcasted_iota(jnp.int32, (tq, tq), 1)
        tri_scr[...] = jnp.where(qry >= key, 0.0, -jnp.inf)
        one_row = lax.broadcasted_iota(jnp.int32, (FOX_VROWS - hd, seq), 0) == 0
        for e in range(2):
            vt_scr[e, hd:FOX_VROWS, :] = jnp.where(one_row, 1.0, 0.0).astype(BF16)

    ka_scr[:, 0:LANES] = k_ref[...]
    sub = lax.broadcasted_iota(jnp.int32, (2 * LANES, 1), 0)
    keep = []
    for e in range(2):
        feat0 = LANES + FOX_FEAT * (2 * pair + e)
        own_q = (sub >= e * FOX_HEAD_DIM) & (sub < (e + 1) * FOX_HEAD_DIM)
        own_feat = (sub >= feat0) & (sub < feat0 + FOX_FEAT)
        keep.append(jnp.where(own_q | own_feat, 1.0, 0.0))
    for blk in range(seq // tq):
        rows = slice(blk * tq, (blk + 1) * tq)
        q_aug = jnp.concatenate([q_ref[rows, :].astype(F32), qf_scr[rows, :].astype(F32)], axis=1)
        q_aug_t = q_aug.T
        v_t = v_ref[rows, :].astype(F32).T.astype(BF16)
        for e in range(2):
            qt_scr[e, :, rows] = (q_aug_t * keep[e]).astype(BF16)
            vt_scr[e, 0:hd, rows] = v_t[e * hd:(e + 1) * hd, :]

    items = []
    for qi in range(seq // tq):
        q0 = qi * tq
        if FOX_TK <= tq:
            spans = [(k0, FOX_TK) for k0 in range(0, q0 + tq, FOX_TK)]
        else:
            spans = [(k0, FOX_TK) for k0 in range(0, q0 - FOX_TK + 1, FOX_TK)]
            k_done = len(spans) * FOX_TK
            spans.append((k_done, q0 + tq - k_done))
        for si, (k0, width) in enumerate(spans):
            for e in range(2):
                items.append((q0, k0, width, k0 + width > q0, e, si == 0, si == len(spans) - 1))
    n_items = len(items)

    def logits_of(idx):
        q0, k0, width, _, e, _, _ = items[idx]
        return _dot(ka_scr[k0:k0 + width, :], qt_scr[e, :, q0:q0 + tq])

    m = {}

    def softmax_of(idx, logit_t):
        q0, k0, width, masked, e, first, _ = items[idx]
        if masked:
            lo = max(q0 - k0, 0)
            diag = logit_t[lo:] + tri_scr[k0 + lo - q0:k0 + width - q0, :]
            logit_t = diag if lo == 0 else jnp.concatenate([logit_t[:lo], diag], axis=0)
        m_blk = jnp.max(logit_t, axis=0, keepdims=True)
        m_new = m_blk if first else jnp.maximum(m[e], m_blk)
        alpha = None if first else jnp.exp2(m[e] - m_new)
        m[e] = m_new
        return alpha, jnp.exp2(logit_t - m_new).astype(BF16)

    acc = {}
    logits = {i: logits_of(i) for i in range(min(2, n_items))}
    probs = {0: softmax_of(0, logits.pop(0))}
    for idx, (q0, k0, width, masked, e, first, last) in enumerate(items):
        if idx + 2 < n_items:
            logits[idx + 2] = logits_of(idx + 2)
        alpha, p_t = probs.pop(idx)
        pv = _dot(vt_scr[e, :, k0:k0 + width], p_t)
        acc[e] = pv if first else alpha * acc[e] + pv
        if idx + 1 < n_items:
            probs[idx + 1] = softmax_of(idx + 1, logits.pop(idx + 1))
        if last and e == 1:
            o_t = jnp.concatenate([acc[h][0:hd] / acc[h][hd:hd + 1] for h in range(2)], axis=0)
            z_ref[q0:q0 + tq, :] = (_sigmoid(gb_ref[q0:q0 + tq, :].astype(F32)) * o_t.T).astype(BF16)


def _fox_call(proj, gates, batch, seq):
    n = batch * seq
    pairs = FOX_HEADS // 2
    per_group = D_MODEL // LANES

    def cols(group):
        return pl.BlockSpec((seq, LANES), lambda b, p: (b, group * per_group + p))

    return pl.pallas_call(
        functools.partial(_fox_kernel, seq=seq),
        grid=(batch, pairs),
        in_specs=[
            cols(COL_FX_Q),
            cols(COL_FX_K),
            cols(COL_FX_V),
            cols(COL_G_B),
            pl.BlockSpec((seq, GATE_COLS), lambda b, p: (b, 0)),
        ],
        out_specs=pl.BlockSpec((seq, LANES), lambda b, p: (b, p)),
        out_shape=jax.ShapeDtypeStruct((n, D_MODEL), BF16),
        scratch_shapes=[
            pltpu.VMEM((seq, LANES), BF16),
            pltpu.VMEM((seq, 2 * LANES), BF16),
            pltpu.VMEM((2, 2 * LANES, seq), BF16),
            pltpu.VMEM((2, FOX_VROWS, seq), BF16),
            pltpu.VMEM((FOX_TQ, FOX_TQ), F32),
        ],
        compiler_params=pltpu.CompilerParams(
            dimension_semantics=("parallel", "arbitrary"), vmem_limit_bytes=VMEM_LIMIT),
        name="fox",
    )(proj, proj, proj, proj, gates)


def _layer(x2d, batch, seq, ffn1_norm, ffn1_w_gate, ffn1_w_up, ffn1_w_down, mix_norm, w_in, b_in,
           conv_w, conv_b, ml_head_norm, w_out, ffn2_norm, ffn2_w_gate, ffn2_w_up, ffn2_w_down, final_g):
    d = D_MODEL
    x1, h1 = _ffn_call(x2d, ffn1_norm[None, :], ffn1_w_gate.astype(BF16), ffn1_w_up.astype(BF16),
                       ffn1_w_down.astype(BF16), next_g=mix_norm[None, :])

    sizes = (d, d, d, d, ML_HEADS, ML_HEADS, d, d, d, FOX_HEADS, d, d)
    starts = [0]
    for s in sizes[:-1]:
        starts.append(starts[-1] + s)
    (ml_q, ml_k, ml_v, ml_o, ml_i, ml_f, fx_q, fx_k, fx_v, fx_f, g_a, g_b) = [
        slice(a, a + s) for a, s in zip(starts, sizes)]
    order = [None] * N_COL_GROUPS
    for col, grp in ((COL_ML_Q, ml_q), (COL_ML_K, ml_k), (COL_ML_V, ml_v), (COL_ML_O, ml_o), (COL_G_A, g_a),
                     (COL_G_B, g_b), (COL_FX_Q, fx_q), (COL_FX_K, fx_k), (COL_FX_V, fx_v)):
        order[col] = grp
    w_main = jnp.concatenate([w_in[:, c] for c in order], axis=1).astype(BF16)
    b_main = jnp.concatenate([b_in[c] for c in order])[None, :]
    col_scale = jnp.ones((N_COL_GROUPS, d), F32).at[COL_FX_Q].set(LOG2E * FOX_HEAD_DIM ** -0.5).reshape(1, -1)
    pad = GATE_COLS - 2 * ML_HEADS - FOX_HEADS
    w_gates = jnp.concatenate([w_in[:, ml_i], w_in[:, ml_f], w_in[:, fx_f],
                               jnp.zeros((d, pad), w_in.dtype)], axis=1).astype(BF16)
    b_gates = jnp.concatenate([b_in[ml_i], b_in[ml_f], b_in[fx_f], jnp.zeros((pad,), b_in.dtype)])[None, :]
    proj, gates = _proj_call(h1, w_main, b_main, col_scale, w_gates, b_gates)

    z_a = _mlstm_call(proj, gates, conv_w, conv_b[None, :], ml_head_norm[None, :], batch, seq)
    z_b = _fox_call(proj, gates, batch, seq)

    return _ffn_call(x1, ffn2_norm[None, :], ffn2_w_gate.astype(BF16), ffn2_w_up.astype(BF16),
                     ffn2_w_down.astype(BF16), pre=(z_a, z_b, w_out.astype(BF16)), final_g=final_g)


def kernel(x, ffn1_norm, ffn1_w_gate, ffn1_w_up, ffn1_w_down, mix_norm, w_in, b_in, conv_w, conv_b,
           ml_head_norm, w_out, ffn2_norm, ffn2_w_gate, ffn2_w_up, ffn2_w_down, final_norm):
    batch, seq, d = x.shape
    assert d == D_MODEL and ffn1_norm.shape[0] == 1, "single-layer configuration"
    out = _layer(x.reshape(batch * seq, d), batch, seq, ffn1_norm[0], ffn1_w_gate[0], ffn1_w_up[0],
                 ffn1_w_down[0], mix_norm[0], w_in[0], b_in[0], conv_w[0], conv_b[0], ml_head_norm[0],
                 w_out[0], ffn2_norm[0], ffn2_w_gate[0], ffn2_w_up[0], ffn2_w_down[0], final_norm[None, :])
    return out.reshape(batch, seq, d)
```

```python
import functools

import jax
import jax.numpy as jnp
from jax import lax
from jax.experimental import pallas as pl
from jax.experimental.pallas import tpu as pltpu

D_MODEL = 1024
ML_HEADS = 4
ML_HEAD_DIM = 256
CONV_WIDTH = 4
FOX_HEADS = 16
FOX_HEAD_DIM = 64
D_FF = 2816
EPS = 1e-6

LANES = 128
GATE_COLS = LANES
ML_I0, ML_F0, FX_F0 = 0, ML_HEADS, 2 * ML_HEADS

(COL_ML_Q, COL_ML_K, COL_ML_V, COL_ML_O, COL_G_A, COL_G_B, COL_FX_Q, COL_FX_K, COL_FX_V) = range(9)
N_COL_GROUPS = 9
LOG2E = 1.4426950408889634

FFN_TM = 512
FFN_TF = 256
PACK_TM = 128
PROJ_TM = 1024
PROJ_TN = 2304
ML_CHUNK = 256
HALO = 16
ML_PAD = 16
FOX_TQ = 256
FOX_TK = 2048
FOX_FEAT = 8
FOX_VROWS = 80
CS_BLK = 256
VMEM_LIMIT = 56 * 1024 * 1024

F32 = jnp.float32
BF16 = jnp.bfloat16


def _dot(a, b):
    return jnp.dot(a, b, preferred_element_type=F32)


def _dot_nt(a, b):
    return lax.dot_general(a, b, (((1,), (1,)), ((), ())), preferred_element_type=F32)


def _dot_tn(a, b):
    return lax.dot_general(a, b, (((0,), (0,)), ((), ())), preferred_element_type=F32)


def _rmsnorm(x, g):
    return x * lax.rsqrt(jnp.mean(x * x, axis=-1, keepdims=True) + EPS) * g


def _sigmoid(x):
    return 1.0 / (1.0 + jnp.exp2(x * (-LOG2E)))


def _log_sigmoid(x):
    return jnp.minimum(x, 0.0) - jnp.log(1.0 + jnp.exp(-jnp.abs(x)))


def _cumsum_rows(x, tril):
    hi = x.astype(BF16)
    r1 = x - hi.astype(F32)
    mid = r1.astype(BF16)
    lo = (r1 - mid.astype(F32)).astype(BF16)
    return _dot(tril, hi) + _dot(tril, mid) + _dot(tril, lo)


def _tril(n):
    r = lax.broadcasted_iota(jnp.int32, (n, n), 0)
    c = lax.broadcasted_iota(jnp.int32, (n, n), 1)
    return r >= c


def _ffn_kernel(*refs, pre_proj, final_norm, next_norm):
    refs = list(refs)
    x_ref = refs.pop(0)
    if pre_proj:
        za_ref, zb_ref, wo_ref = refs.pop(0), refs.pop(0), refs.pop(0)
    g_ref, wg_ref, wu_ref, wd_ref = refs.pop(0), refs.pop(0), refs.pop(0), refs.pop(0)
    if final_norm:
        gf_ref = refs.pop(0)
    if next_norm:
        gn_ref = refs.pop(0)
    o_ref = refs.pop(0)
    if next_norm:
        hn_ref = refs.pop(0)
    (a_scr,) = refs

    x = x_ref[...]
    if pre_proj:
        y = (za_ref[...].astype(F32) + zb_ref[...].astype(F32)).astype(BF16)
        x = x + _dot(y, wo_ref[...])
    h = _rmsnorm(x, g_ref[...]).astype(BF16)
    for j in range(D_FF // FFN_TF):
        cols = slice(j * FFN_TF, (j + 1) * FFN_TF)
        gate = _dot(h, wg_ref[:, cols])
        up = _dot(h, wu_ref[:, cols])
        a_scr[:, cols] = (gate * _sigmoid(gate) * up).astype(BF16)
    out = x + 0.5 * _dot(a_scr[...], wd_ref[...])
    if next_norm:
        hn_ref[...] = _rmsnorm(out, gn_ref[...]).astype(BF16)
    if final_norm:
        out = _rmsnorm(out, gf_ref[...])
    o_ref[...] = out


def _ffn_call(x, norm_g, wg, wu, wd, pre=None, final_g=None, next_g=None):
    n = x.shape[0]
    row = pl.BlockSpec((FFN_TM, D_MODEL), lambda i: (i, 0))
    vec = pl.BlockSpec((1, D_MODEL), lambda i: (0, 0))
    out_specs, out_shape = row, jax.ShapeDtypeStruct((n, D_MODEL), F32)
    if next_g is not None:
        out_specs, out_shape = [row, row], [out_shape, jax.ShapeDtypeStruct((n, D_MODEL), BF16)]

    def resident(shape):
        return pl.BlockSpec(shape, lambda i: (0, 0), pipeline_mode=pl.Buffered(1))

    args, specs = [x], [row]
    if pre is not None:
        za, zb, wo = pre
        args += [za, zb, wo]
        specs += [row, row, resident((D_MODEL, D_MODEL))]
    args += [norm_g, wg, wu, wd]
    specs += [vec, resident((D_MODEL, D_FF)), resident((D_MODEL, D_FF)), resident((D_FF, D_MODEL))]
    for g in (final_g, next_g):
        if g is not None:
            args.append(g)
            specs.append(vec)
    return pl.pallas_call(
        functools.partial(_ffn_kernel, pre_proj=pre is not None, final_norm=final_g is not None,
                          next_norm=next_g is not None),
        grid=(n // FFN_TM,),
        in_specs=specs,
        out_specs=out_specs,
        out_shape=out_shape,
        scratch_shapes=[pltpu.VMEM((FFN_TM, D_FF), BF16)],
        compiler_params=pltpu.CompilerParams(
            dimension_semantics=("parallel",), vmem_limit_bytes=VMEM_LIMIT),
        name="ffn2" if pre is not None else "ffn1",
    )(*args)


def _pack_w_kernel(w_ref, o_ref, *, starts):
    for g, start in enumerate(starts):
        o_ref[:, g * D_MODEL:(g + 1) * D_MODEL] = w_ref[:, start:start + D_MODEL].astype(BF16)


def _pack_w_call(w_in, starts):
    d, n_in = w_in.shape
    return pl.pallas_call(
        functools.partial(_pack_w_kernel, starts=starts),
        grid=(d // PACK_TM,),
        in_specs=[pl.BlockSpec((PACK_TM, n_in), lambda i: (i, 0))],
        out_specs=pl.BlockSpec((PACK_TM, len(starts) * D_MODEL), lambda i: (i, 0)),
        out_shape=jax.ShapeDtypeStruct((d, len(starts) * D_MODEL), BF16),
        compiler_params=pltpu.CompilerParams(
            dimension_semantics=("parallel",), vmem_limit_bytes=VMEM_LIMIT),
        name="pack_w_in",
    )(w_in)


def _proj_kernel(h_ref, w_ref, b_ref, s_ref, wgt_ref, bgt_ref, p_ref, gt_ref):
    h = h_ref[...]

    @pl.when(pl.program_id(1) == 0)
    def _():
        gt_ref[...] = _dot(h, wgt_ref[...]) + bgt_ref[...]

    p_ref[...] = ((_dot(h, w_ref[...]) + b_ref[...]) * s_ref[...]).astype(BF16)


def _proj_call(h, w, b, col_scale, w_gates, b_gates):
    n = h.shape[0]
    n_out = w.shape[1]
    return pl.pallas_call(
        _proj_kernel,
        grid=(n // PROJ_TM, n_out // PROJ_TN),
        in_specs=[
            pl.BlockSpec((PROJ_TM, D_MODEL), lambda i, j: (i, 0)),
            pl.BlockSpec((D_MODEL, PROJ_TN), lambda i, j: (0, j)),
            pl.BlockSpec((1, PROJ_TN), lambda i, j: (0, j)),
            pl.BlockSpec((1, PROJ_TN), lambda i, j: (0, j)),
            pl.BlockSpec((D_MODEL, GATE_COLS), lambda i, j: (0, 0)),
            pl.BlockSpec((1, GATE_COLS), lambda i, j: (0, 0)),
        ],
        out_specs=[
            pl.BlockSpec((PROJ_TM, PROJ_TN), lambda i, j: (i, j)),
            pl.BlockSpec((PROJ_TM, GATE_COLS), lambda i, j: (i, 0)),
        ],
        out_shape=[
            jax.ShapeDtypeStruct((n, n_out), BF16),
            jax.ShapeDtypeStruct((n, GATE_COLS), F32),
        ],
        compiler_params=pltpu.CompilerParams(
            dimension_semantics=("parallel", "arbitrary"), vmem_limit_bytes=VMEM_LIMIT),
        name="in_proj",
    )(h, w, b, col_scale, w_gates, b_gates)


def _mlstm_kernel(qk_ref, o_ref, ga_ref, v_ref, gt_ref, cw_ref, cb_ref, ln_ref, z_ref,
                  halo_scr, c_scr, m_scr):
    L = ML_CHUNK
    dh = ML_HEAD_DIM
    rows = dh + ML_PAD

    @pl.when(pl.program_id(1) == 0)
    def _():
        halo_scr[...] = jnp.zeros_like(halo_scr)
        c_scr[...] = jnp.zeros_like(c_scr)
        m_scr[...] = jnp.zeros_like(m_scr)

    u16 = qk_ref[...]
    halo16 = halo_scr[...]
    t_out = lax.broadcasted_iota(jnp.int32, (L, L), 0)
    t_in = lax.broadcasted_iota(jnp.int32, (L, L), 1)
    h_out = lax.broadcasted_iota(jnp.int32, (HALO, HALO), 0)
    h_in = lax.broadcasted_iota(jnp.int32, (HALO, HALO), 1)
    y = cb_ref[...] + cw_ref[CONV_WIDTH - 1:CONV_WIDTH, :] * u16.astype(F32)
    head_fix = jnp.zeros((HALO, 2 * D_MODEL), F32)
    for d in range(1, CONV_WIDTH):
        w_d = cw_ref[CONV_WIDTH - 1 - d:CONV_WIDTH - d, :]
        y = y + w_d * _dot(jnp.where(t_in == t_out - d, 1.0, 0.0).astype(BF16), u16)
        head_fix = head_fix + w_d * _dot(jnp.where(h_in == h_out - d + HALO, 1.0, 0.0).astype(BF16), halo16)
    y = jnp.concatenate([y[0:HALO] + head_fix, y[HALO:]], axis=0)
    halo_scr[...] = u16[L - HALO:L, :]
    qk = y * _sigmoid(y)

    gates = gt_ref[...]
    tril = _tril(L)
    bcum = _cumsum_rows(_log_sigmoid(gates), tril.astype(BF16))
    gates_t = gates.T
    bcum_t = bcum.T
    pad_row = lax.broadcasted_iota(jnp.int32, (ML_PAD, L), 0)
    ones_pad = jnp.where(pad_row == 0, 1.0, 0.0).astype(BF16)

    def matmul_stage(h):
        q_t = qk[:, h * dh:(h + 1) * dh].T.astype(BF16)
        k16 = (qk[:, D_MODEL + h * dh:D_MODEL + (h + 1) * dh] * (dh ** -0.5)).astype(BF16)
        state = c_scr[h]
        return q_t, k16, state, _dot(k16, q_t), _dot(state.astype(BF16), q_t)

    staged = matmul_stage(0)
    for h in range(ML_HEADS):
        cols = slice(h * dh, (h + 1) * dh)
        q_t, k16, state, qk_t, inter_t = staged
        if h + 1 < ML_HEADS:
            staged = matmul_stage(h + 1)
        v_aug = jnp.concatenate([v_ref[:, cols].astype(F32).T.astype(BF16), ones_pad], axis=0)
        li_row = gates_t[ML_I0 + h:ML_I0 + h + 1, :]
        b_row = bcum_t[ML_F0 + h:ML_F0 + h + 1, :]
        ib_col = gates[:, ML_I0 + h:ML_I0 + h + 1] - bcum[:, ML_F0 + h:ML_F0 + h + 1]
        m_prev = m_scr[h][:, 0:1]

        dmat_t = jnp.where(t_out <= t_in, b_row + ib_col, -jnp.inf)
        inter = b_row + m_prev
        m_t = jnp.maximum(jnp.max(dmat_t, axis=0, keepdims=True), inter)
        w_inter = jnp.exp(inter - m_t)
        s_t = (qk_t * jnp.exp(dmat_t - m_t)).astype(BF16)
        num_den = _dot(v_aug, s_t) + w_inter * inter_t
        den = num_den[dh:dh + 1, :]
        hh_t = num_den[0:dh, :] / jnp.maximum(jnp.abs(den), jnp.exp(-m_t))

        b_last = b_row[:, L - 1:L]
        log_w = b_last - b_row + li_row
        m_new = jnp.maximum(b_last + m_prev, jnp.max(log_w, axis=1, keepdims=True))
        decay = jnp.exp(b_last + m_prev - m_new)
        w = jnp.exp(log_w - m_new)
        c_scr[h] = decay * state + _dot((v_aug.astype(F32) * w).astype(BF16), k16)
        m_scr[h] = jnp.broadcast_to(m_new, (1, LANES))

        mu = jnp.mean(hh_t, axis=0, keepdims=True)
        dlt = hh_t - mu
        var = jnp.mean(dlt * dlt, axis=0, keepdims=True)
        hn = (dlt * lax.rsqrt(var + EPS)).T * ln_ref[:, cols]
        gate = _sigmoid(ga_ref[:, cols].astype(F32)) * _sigmoid(o_ref[:, cols].astype(F32))
        z_ref[:, cols] = (gate * hn).astype(BF16)


def _mlstm_call(proj, gates, conv_w, conv_b, ln_g, batch, seq):
    n = batch * seq
    nc = seq // ML_CHUNK
    L = ML_CHUNK

    def row(width, col):
        return pl.BlockSpec((L, width), lambda b, c: (b * nc + c, col))

    def full(shape):
        return pl.BlockSpec(shape, lambda b, c: (0, 0))

    return pl.pallas_call(
        _mlstm_kernel,
        grid=(batch, nc),
        in_specs=[
            row(2 * D_MODEL, COL_ML_Q // 2),
            row(D_MODEL, COL_ML_O),
            row(D_MODEL, COL_G_A),
            row(D_MODEL, COL_ML_V),
            row(GATE_COLS, 0),
            full((CONV_WIDTH, 2 * D_MODEL)),
            full((1, 2 * D_MODEL)),
            full((1, D_MODEL)),
        ],
        out_specs=row(D_MODEL, 0),
        out_shape=jax.ShapeDtypeStruct((n, D_MODEL), BF16),
        scratch_shapes=[
            pltpu.VMEM((HALO, 2 * D_MODEL), BF16),
            pltpu.VMEM((ML_HEADS, ML_HEAD_DIM + ML_PAD, ML_HEAD_DIM), F32),
            pltpu.VMEM((ML_HEADS, 1, LANES), F32),
        ],
        compiler_params=pltpu.CompilerParams(
            dimension_semantics=("parallel", "arbitrary"), vmem_limit_bytes=VMEM_LIMIT),
        name="mlstm",
    )(proj, proj, proj, proj, gates, conv_w, conv_b, ln_g)


def _fox_bias_features(gt_ref, qf_scr, ka_scr, seq):
    tril = _tril(CS_BLK).astype(BF16)
    r = lax.broadcasted_iota(jnp.int32, (GATE_COLS, 2 * LANES), 0)
    c = lax.broadcasted_iota(jnp.int32, (GATE_COLS, 2 * LANES), 1)
    head = r - FX_F0
    is_head = (head >= 0) & (head < FOX_HEADS)
    sel = []
    for piece in range(3):
        q_hit = is_head & (c == FOX_FEAT * head + piece)
        k_hit = is_head & (c == LANES + FOX_FEAT * head + 3 + piece)
        sel.append(jnp.where(q_hit, 1.0, jnp.where(k_hit, -1.0, 0.0)).astype(BF16))
    lane = lax.broadcasted_iota(jnp.int32, (1, 2 * LANES), 1)
    slot = lane % FOX_FEAT
    in_use = (lane % LANES) < FOX_FEAT * FOX_HEADS
    q_one = (lane < LANES) & (slot >= 3) & (slot < 6)
    k_one = (lane >= LANES) & (slot < 3)
    ones = jnp.where(in_use & (q_one | k_one), 1.0, 0.0)

    carry = jnp.zeros((1, GATE_COLS), F32)
    for blk in range(seq // CS_BLK):
        rows = slice(blk * CS_BLK, (blk + 1) * CS_BLK)
        cs = _cumsum_rows(_log_sigmoid(gt_ref[rows, :]), tril) + carry
        carry = cs[CS_BLK - 1:CS_BLK, :]
        cs2 = cs * LOG2E
        hi = cs2.astype(BF16)
        r1 = cs2 - hi.astype(F32)
        mid = r1.astype(BF16)
        lo = (r1 - mid.astype(F32)).astype(BF16)
        feat = (_dot(hi, sel[0]) + _dot(mid, sel[1]) + _dot(lo, sel[2]) + ones).astype(BF16)
        qf_scr[rows, :] = feat[:, 0:LANES]
        ka_scr[rows, LANES:2 * LANES] = feat[:, LANES:2 * LANES]


def _fox_kernel(q_ref, k_ref, v_ref, gb_ref, gt_ref, z_ref, qf_scr, ka_scr, qt_scr, vt_scr, tri_scr, *, seq):
    tq = FOX_TQ
    hd = FOX_HEAD_DIM
    pair = pl.program_id(1)

    @pl.when(pair == 0)
    def _():
        _fox_bias_features(gt_ref, qf_scr, ka_scr, seq)
        key = lax.broadcasted_iota(jnp.int32, (tq, tq), 0)
        qry = lax.broadcasted_iota(jnp.int32, (tq, tq), 1)
        tri_scr[...] = jnp.where(qry >= key, 0.0, -jnp.inf)
        one_row = lax.broadcasted_iota(jnp.int32, (FOX_VROWS - hd, seq), 0) == 0
        for e in range(2):
            vt_scr[e, hd:FOX_VROWS, :] = jnp.where(one_row, 1.0, 0.0).astype(BF16)

    ka_scr[:, 0:LANES] = k_ref[...]
    sub = lax.broadcasted_iota(jnp.int32, (2 * LANES, 1), 0)
    keep = []
    for e in range(2):
        feat0 = LANES + FOX_FEAT * (2 * pair + e)
        own_q = (sub >= e * FOX_HEAD_DIM) & (sub < (e + 1) * FOX_HEAD_DIM)
        own_feat = (sub >= feat0) & (sub < feat0 + FOX_FEAT)
        keep.append(jnp.where(own_q | own_feat, 1.0, 0.0))
    for blk in range(seq // tq):
        rows = slice(blk * tq, (blk + 1) * tq)
        q_aug = jnp.concatenate([q_ref[rows, :].astype(F32), qf_scr[rows, :].astype(F32)], axis=1)
        q_aug_t = q_aug.T
        v_t = v_ref[rows, :].astype(F32).T.astype(BF16)
        for e in range(2):
            qt_scr[e, :, rows] = (q_aug_t * keep[e]).astype(BF16)
            vt_scr[e, 0:hd, rows] = v_t[e * hd:(e + 1) * hd, :]

    items = []
    for qi in range(seq // tq):
        q0 = qi * tq
        if FOX_TK <= tq:
            spans = [(k0, FOX_TK) for k0 in range(0, q0 + tq, FOX_TK)]
        else:
            spans = [(k0, FOX_TK) for k0 in range(0, q0 - FOX_TK + 1, FOX_TK)]
            k_done = len(spans) * FOX_TK
            spans.append((k_done, q0 + tq - k_done))
        for si, (k0, width) in enumerate(spans):
            for e in range(2):
                items.append((q0, k0, width, k0 + width > q0, e, si == 0, si == len(spans) - 1))
    n_items = len(items)

    def logits_of(idx):
        q0, k0, width, _, e, _, _ = items[idx]
        return _dot(ka_scr[k0:k0 + width, :], qt_scr[e, :, q0:q0 + tq])

    m = {}

    def softmax_of(idx, logit_t):
        q0, k0, width, masked, e, first, _ = items[idx]
        if masked:
            lo = max(q0 - k0, 0)
            diag = logit_t[lo:] + tri_scr[k0 + lo - q0:k0 + width - q0, :]
            logit_t = diag if lo == 0 else jnp.concatenate([logit_t[:lo], diag], axis=0)
        m_blk = jnp.max(logit_t, axis=0, keepdims=True)
        m_new = m_blk if first else jnp.maximum(m[e], m_blk)
        alpha = None if first else jnp.exp2(m[e] - m_new)
        m[e] = m_new
        return alpha, jnp.exp2(logit_t - m_new).astype(BF16)

    acc = {}
    logits = {i: logits_of(i) for i in range(min(2, n_items))}
    probs = {0: softmax_of(0, logits.pop(0))}
    for idx, (q0, k0, width, masked, e, first, last) in enumerate(items):
        if idx + 2 < n_items:
            logits[idx + 2] = logits_of(idx + 2)
        if idx + 1 < n_items:
            probs[idx + 1] = softmax_of(idx + 1, logits.pop(idx + 1))
        alpha, p_t = probs.pop(idx)
        pv = _dot(vt_scr[e, :, k0:k0 + width], p_t)
        acc[e] = pv if first else alpha * acc[e] + pv
        if last and e == 1:
            o_t = jnp.concatenate([acc[h][0:hd] / acc[h][hd:hd + 1] for h in range(2)], axis=0)
            z_ref[q0:q0 + tq, :] = (_sigmoid(gb_ref[q0:q0 + tq, :].astype(F32)) * o_t.T).astype(BF16)


def _fox_call(proj, gates, batch, seq):
    n = batch * seq
    pairs = FOX_HEADS // 2
    per_group = D_MODEL // LANES

    def cols(group):
        return pl.BlockSpec((seq, LANES), lambda b, p: (b, group * per_group + p))

    return pl.pallas_call(
        functools.partial(_fox_kernel, seq=seq),
        grid=(batch, pairs),
        in_specs=[
            cols(COL_FX_Q),
            cols(COL_FX_K),
            cols(COL_FX_V),
            cols(COL_G_B),
            pl.BlockSpec((seq, GATE_COLS), lambda b, p: (b, 0)),
        ],
        out_specs=pl.BlockSpec((seq, LANES), lambda b, p: (b, p)),
        out_shape=jax.ShapeDtypeStruct((n, D_MODEL), BF16),
        scratch_shapes=[
            pltpu.VMEM((seq, LANES), BF16),
            pltpu.VMEM((seq, 2 * LANES), BF16),
            pltpu.VMEM((2, 2 * LANES, seq), BF16),
            pltpu.VMEM((2, FOX_VROWS, seq), BF16),
            pltpu.VMEM((FOX_TQ, FOX_TQ), F32),
        ],
        compiler_params=pltpu.CompilerParams(
            dimension_semantics=("parallel", "arbitrary"), vmem_limit_bytes=VMEM_LIMIT),
        name="fox",
    )(proj, proj, proj, proj, gates)


def _layer(x2d, batch, seq, ffn1_norm, ffn1_w_gate, ffn1_w_up, ffn1_w_down, mix_norm, w_in, b_in,
           conv_w, conv_b, ml_head_norm, w_out, ffn2_norm, ffn2_w_gate, ffn2_w_up, ffn2_w_down, final_g):
    d = D_MODEL
    x1, h1 = _ffn_call(x2d, ffn1_norm[None, :], ffn1_w_gate.astype(BF16), ffn1_w_up.astype(BF16),
                       ffn1_w_down.astype(BF16), next_g=mix_norm[None, :])

    sizes = (d, d, d, d, ML_HEADS, ML_HEADS, d, d, d, FOX_HEADS, d, d)
    starts = [0]
    for s in sizes[:-1]:
        starts.append(starts[-1] + s)
    (ml_q, ml_k, ml_v, ml_o, ml_i, ml_f, fx_q, fx_k, fx_v, fx_f, g_a, g_b) = [
        slice(a, a + s) for a, s in zip(starts, sizes)]
    order = [None] * N_COL_GROUPS
    for col, grp in ((COL_ML_Q, ml_q), (COL_ML_K, ml_k), (COL_ML_V, ml_v), (COL_ML_O, ml_o), (COL_G_A, g_a),
                     (COL_G_B, g_b), (COL_FX_Q, fx_q), (COL_FX_K, fx_k), (COL_FX_V, fx_v)):
        order[col] = grp
    w_main = _pack_w_call(w_in, tuple(c.start for c in order))
    b_main = jnp.concatenate([b_in[c] for c in order])[None, :]
    col_scale = jnp.ones((N_COL_GROUPS, d), F32).at[COL_FX_Q].set(LOG2E * FOX_HEAD_DIM ** -0.5).reshape(1, -1)
    pad = GATE_COLS - 2 * ML_HEADS - FOX_HEADS
    w_gates = jnp.concatenate([w_in[:, ml_i], w_in[:, ml_f], w_in[:, fx_f],
                               jnp.zeros((d, pad), w_in.dtype)], axis=1).astype(BF16)
    b_gates = jnp.concatenate([b_in[ml_i], b_in[ml_f], b_in[fx_f], jnp.zeros((pad,), b_in.dtype)])[None, :]
    proj, gates = _proj_call(h1, w_main, b_main, col_scale, w_gates, b_gates)

    z_a = _mlstm_call(proj, gates, conv_w, conv_b[None, :], ml_head_norm[None, :], batch, seq)
    z_b = _fox_call(proj, gates, batch, seq)

    return _ffn_call(x1, ffn2_norm[None, :], ffn2_w_gate.astype(BF16), ffn2_w_up.astype(BF16),
                     ffn2_w_down.astype(BF16), pre=(z_a, z_b, w_out.astype(BF16)), final_g=final_g)


def kernel(x, ffn1_norm, ffn1_w_gate, ffn1_w_up, ffn1_w_down, mix_norm, w_in, b_in, conv_w, conv_b,
           ml_head_norm, w_out, ffn2_norm, ffn2_w_gate, ffn2_w_up, ffn2_w_down, final_norm):
    batch, seq, d = x.shape
    assert d == D_MODEL and ffn1_norm.shape[0] == 1, "single-layer configuration"
    out = _layer(x.reshape(batch * seq, d), batch, seq, ffn1_norm[0], ffn1_w_gate[0], ffn1_w_up[0],
                 ffn1_w_down[0], mix_norm[0], w_in[0], b_in[0], conv_w[0], conv_b[0], ml_head_norm[0],
                 w_out[0], ffn2_norm[0], ffn2_w_gate[0], ffn2_w_up[0], ffn2_w_down[0], final_norm[None, :])
    return out.reshape(batch, seq, d)
```

```python
import functools

import jax
import jax.numpy as jnp
from jax import lax
from jax.experimental import pallas as pl
from jax.experimental.pallas import tpu as pltpu

D_MODEL = 1024
ML_HEADS = 4
ML_HEAD_DIM = 256
CONV_WIDTH = 4
FOX_HEADS = 16
FOX_HEAD_DIM = 64
D_FF = 2816
EPS = 1e-6

LANES = 128
GATE_COLS = LANES
ML_I0, ML_F0, FX_F0 = 0, ML_HEADS, 2 * ML_HEADS

(COL_ML_Q, COL_ML_K, COL_ML_V, COL_ML_O, COL_G_A, COL_G_B, COL_FX_Q, COL_FX_K, COL_FX_V) = range(9)
N_COL_GROUPS = 9
LOG2E = 1.4426950408889634

FFN_TM = 512
FFN_TF = 256
PROJ_TM = 1024
PROJ_TN = 2304
ML_CHUNK = 256
HALO = 16
ML_PAD = 16
FOX_TQ = 256
FOX_TK = 2048
FOX_FEAT = 8
FOX_VROWS = 80
CS_BLK = 256
VMEM_LIMIT = 56 * 1024 * 1024

F32 = jnp.float32
BF16 = jnp.bfloat16


def _dot(a, b):
    return jnp.dot(a, b, preferred_element_type=F32)


def _dot_nt(a, b):
    return lax.dot_general(a, b, (((1,), (1,)), ((), ())), preferred_element_type=F32)


def _dot_tn(a, b):
    return lax.dot_general(a, b, (((0,), (0,)), ((), ())), preferred_element_type=F32)


def _rmsnorm(x, g):
    return x * lax.rsqrt(jnp.mean(x * x, axis=-1, keepdims=True) + EPS) * g


def _sigmoid(x):
    return 1.0 / (1.0 + jnp.exp2(x * (-LOG2E)))


def _log_sigmoid(x):
    return jnp.minimum(x, 0.0) - jnp.log(1.0 + jnp.exp(-jnp.abs(x)))


def _cumsum_rows(x, tril):
    hi = x.astype(BF16)
    r1 = x - hi.astype(F32)
    mid = r1.astype(BF16)
    lo = (r1 - mid.astype(F32)).astype(BF16)
    return _dot(tril, hi) + _dot(tril, mid) + _dot(tril, lo)


def _tril(n):
    r = lax.broadcasted_iota(jnp.int32, (n, n), 0)
    c = lax.broadcasted_iota(jnp.int32, (n, n), 1)
    return r >= c


def _ffn_kernel(*refs, pre_proj, final_norm, next_norm):
    refs = list(refs)
    x_ref = refs.pop(0)
    if pre_proj:
        za_ref, zb_ref, wo_ref = refs.pop(0), refs.pop(0), refs.pop(0)
    g_ref, wg_ref, wu_ref, wd_ref = refs.pop(0), refs.pop(0), refs.pop(0), refs.pop(0)
    if final_norm:
        gf_ref = refs.pop(0)
    if next_norm:
        gn_ref = refs.pop(0)
    o_ref = refs.pop(0)
    if next_norm:
        hn_ref = refs.pop(0)
    (a_scr,) = refs

    x = x_ref[...]
    if pre_proj:
        y = (za_ref[...].astype(F32) + zb_ref[...].astype(F32)).astype(BF16)
        x = x + _dot(y, wo_ref[...])
    h = _rmsnorm(x, g_ref[...]).astype(BF16)
    for j in range(D_FF // FFN_TF):
        cols = slice(j * FFN_TF, (j + 1) * FFN_TF)
        gate = _dot(h, wg_ref[:, cols])
        up = _dot(h, wu_ref[:, cols])
        a_scr[:, cols] = (gate * _sigmoid(gate) * up).astype(BF16)
    out = x + 0.5 * _dot(a_scr[...], wd_ref[...])
    if next_norm:
        hn_ref[...] = _rmsnorm(out, gn_ref[...]).astype(BF16)
    if final_norm:
        out = _rmsnorm(out, gf_ref[...])
    o_ref[...] = out


def _ffn_call(x, norm_g, wg, wu, wd, pre=None, final_g=None, next_g=None):
    n = x.shape[0]
    row = pl.BlockSpec((FFN_TM, D_MODEL), lambda i: (i, 0))
    vec = pl.BlockSpec((1, D_MODEL), lambda i: (0, 0))
    out_specs, out_shape = row, jax.ShapeDtypeStruct((n, D_MODEL), F32)
    if next_g is not None:
        out_specs, out_shape = [row, row], [out_shape, jax.ShapeDtypeStruct((n, D_MODEL), BF16)]

    def resident(shape):
        return pl.BlockSpec(shape, lambda i: (0, 0), pipeline_mode=pl.Buffered(1))

    args, specs = [x], [row]
    if pre is not None:
        za, zb, wo = pre
        args += [za, zb, wo]
        specs += [row, row, resident((D_MODEL, D_MODEL))]
    args += [norm_g, wg, wu, wd]
    specs += [vec, resident((D_MODEL, D_FF)), resident((D_MODEL, D_FF)), resident((D_FF, D_MODEL))]
    for g in (final_g, next_g):
        if g is not None:
            args.append(g)
            specs.append(vec)
    return pl.pallas_call(
        functools.partial(_ffn_kernel, pre_proj=pre is not None, final_norm=final_g is not None,
                          next_norm=next_g is not None),
        grid=(n // FFN_TM,),
        in_specs=specs,
        out_specs=out_specs,
        out_shape=out_shape,
        scratch_shapes=[pltpu.VMEM((FFN_TM, D_FF), BF16)],
        compiler_params=pltpu.CompilerParams(
            dimension_semantics=("parallel",), vmem_limit_bytes=VMEM_LIMIT),
        name="ffn2" if pre is not None else "ffn1",
    )(*args)


def _proj_kernel(h_ref, w_ref, b_ref, s_ref, wgt_ref, bgt_ref, p_ref, gt_ref):
    h = h_ref[...]

    @pl.when(pl.program_id(1) == 0)
    def _():
        gt_ref[...] = _dot_nt(h, wgt_ref[...]) + bgt_ref[...]

    p_ref[...] = ((_dot_nt(h, w_ref[...]) + b_ref[...]) * s_ref[...]).astype(BF16)


def _proj_call(h, w_t, b, col_scale, w_gates_t, b_gates):
    n = h.shape[0]
    n_out = w_t.shape[0]
    return pl.pallas_call(
        _proj_kernel,
        grid=(n // PROJ_TM, n_out // PROJ_TN),
        in_specs=[
            pl.BlockSpec((PROJ_TM, D_MODEL), lambda i, j: (i, 0)),
            pl.BlockSpec((PROJ_TN, D_MODEL), lambda i, j: (j, 0)),
            pl.BlockSpec((1, PROJ_TN), lambda i, j: (0, j)),
            pl.BlockSpec((1, PROJ_TN), lambda i, j: (0, j)),
            pl.BlockSpec((GATE_COLS, D_MODEL), lambda i, j: (0, 0)),
            pl.BlockSpec((1, GATE_COLS), lambda i, j: (0, 0)),
        ],
        out_specs=[
            pl.BlockSpec((PROJ_TM, PROJ_TN), lambda i, j: (i, j)),
            pl.BlockSpec((PROJ_TM, GATE_COLS), lambda i, j: (i, 0)),
        ],
        out_shape=[
            jax.ShapeDtypeStruct((n, n_out), BF16),
            jax.ShapeDtypeStruct((n, GATE_COLS), F32),
        ],
        compiler_params=pltpu.CompilerParams(
            dimension_semantics=("parallel", "arbitrary"), vmem_limit_bytes=VMEM_LIMIT),
        name="in_proj",
    )(h, w_t, b, col_scale, w_gates_t, b_gates)


def _mlstm_kernel(qk_ref, o_ref, ga_ref, v_ref, gt_ref, cw_ref, cb_ref, ln_ref, z_ref,
                  halo_scr, c_scr, m_scr):
    L = ML_CHUNK
    dh = ML_HEAD_DIM
    rows = dh + ML_PAD

    @pl.when(pl.program_id(1) == 0)
    def _():
        halo_scr[...] = jnp.zeros_like(halo_scr)
        c_scr[...] = jnp.zeros_like(c_scr)
        m_scr[...] = jnp.zeros_like(m_scr)

    u16 = qk_ref[...]
    halo16 = halo_scr[...]
    t_out = lax.broadcasted_iota(jnp.int32, (L, L), 0)
    t_in = lax.broadcasted_iota(jnp.int32, (L, L), 1)
    h_out = lax.broadcasted_iota(jnp.int32, (HALO, HALO), 0)
    h_in = lax.broadcasted_iota(jnp.int32, (HALO, HALO), 1)
    y = cb_ref[...] + cw_ref[CONV_WIDTH - 1:CONV_WIDTH, :] * u16.astype(F32)
    head_fix = jnp.zeros((HALO, 2 * D_MODEL), F32)
    for d in range(1, CONV_WIDTH):
        w_d = cw_ref[CONV_WIDTH - 1 - d:CONV_WIDTH - d, :]
        y = y + w_d * _dot(jnp.where(t_in == t_out - d, 1.0, 0.0).astype(BF16), u16)
        head_fix = head_fix + w_d * _dot(jnp.where(h_in == h_out - d + HALO, 1.0, 0.0).astype(BF16), halo16)
    y = jnp.concatenate([y[0:HALO] + head_fix, y[HALO:]], axis=0)
    halo_scr[...] = u16[L - HALO:L, :]
    qk = y * _sigmoid(y)

    gates = gt_ref[...]
    tril = _tril(L)
    bcum = _cumsum_rows(_log_sigmoid(gates), tril.astype(BF16))
    gates_t = gates.T
    bcum_t = bcum.T
    pad_row = lax.broadcasted_iota(jnp.int32, (ML_PAD, L), 0)
    ones_pad = jnp.where(pad_row == 0, 1.0, 0.0).astype(BF16)

    def matmul_stage(h):
        q_t = qk[:, h * dh:(h + 1) * dh].T.astype(BF16)
        k16 = (qk[:, D_MODEL + h * dh:D_MODEL + (h + 1) * dh] * (dh ** -0.5)).astype(BF16)
        state = c_scr[h]
        return q_t, k16, state, _dot(k16, q_t), _dot(state.astype(BF16), q_t)

    staged = matmul_stage(0)
    for h in range(ML_HEADS):
        cols = slice(h * dh, (h + 1) * dh)
        q_t, k16, state, qk_t, inter_t = staged
        if h + 1 < ML_HEADS:
            staged = matmul_stage(h + 1)
        v_aug = jnp.concatenate([v_ref[:, cols].astype(F32).T.astype(BF16), ones_pad], axis=0)
        li_row = gates_t[ML_I0 + h:ML_I0 + h + 1, :]
        b_row = bcum_t[ML_F0 + h:ML_F0 + h + 1, :]
        ib_col = gates[:, ML_I0 + h:ML_I0 + h + 1] - bcum[:, ML_F0 + h:ML_F0 + h + 1]
        m_prev = m_scr[h][:, 0:1]

        dmat_t = jnp.where(t_out <= t_in, b_row + ib_col, -jnp.inf)
        inter = b_row + m_prev
        m_t = jnp.maximum(jnp.max(dmat_t, axis=0, keepdims=True), inter)
        w_inter = jnp.exp(inter - m_t)
        s_t = (qk_t * jnp.exp(dmat_t - m_t)).astype(BF16)
        num_den = _dot(v_aug, s_t) + w_inter * inter_t
        den = num_den[dh:dh + 1, :]
        hh_t = num_den[0:dh, :] / jnp.maximum(jnp.abs(den), jnp.exp(-m_t))

        b_last = b_row[:, L - 1:L]
        log_w = b_last - b_row + li_row
        m_new = jnp.maximum(b_last + m_prev, jnp.max(log_w, axis=1, keepdims=True))
        decay = jnp.exp(b_last + m_prev - m_new)
        w = jnp.exp(log_w - m_new)
        c_scr[h] = decay * state + _dot((v_aug.astype(F32) * w).astype(BF16), k16)
        m_scr[h] = jnp.broadcast_to(m_new, (1, LANES))

        mu = jnp.mean(hh_t, axis=0, keepdims=True)
        dlt = hh_t - mu
        var = jnp.mean(dlt * dlt, axis=0, keepdims=True)
        hn = (dlt * lax.rsqrt(var + EPS)).T * ln_ref[:, cols]
        gate = _sigmoid(ga_ref[:, cols].astype(F32)) * _sigmoid(o_ref[:, cols].astype(F32))
        z_ref[:, cols] = (gate * hn).astype(BF16)


def _mlstm_call(proj, gates, conv_w, conv_b, ln_g, batch, seq):
    n = batch * seq
    nc = seq // ML_CHUNK
    L = ML_CHUNK

    def row(width, col):
        return pl.BlockSpec((L, width), lambda b, c: (b * nc + c, col))

    def full(shape):
        return pl.BlockSpec(shape, lambda b, c: (0, 0))

    return pl.pallas_call(
        _mlstm_kernel,
        grid=(batch, nc),
        in_specs=[
            row(2 * D_MODEL, COL_ML_Q // 2),
            row(D_MODEL, COL_ML_O),
            row(D_MODEL, COL_G_A),
            row(D_MODEL, COL_ML_V),
            row(GATE_COLS, 0),
            full((CONV_WIDTH, 2 * D_MODEL)),
            full((1, 2 * D_MODEL)),
            full((1, D_MODEL)),
        ],
        out_specs=row(D_MODEL, 0),
        out_shape=jax.ShapeDtypeStruct((n, D_MODEL), BF16),
        scratch_shapes=[
            pltpu.VMEM((HALO, 2 * D_MODEL), BF16),
            pltpu.VMEM((ML_HEADS, ML_HEAD_DIM + ML_PAD, ML_HEAD_DIM), F32),
            pltpu.VMEM((ML_HEADS, 1, LANES), F32),
        ],
        compiler_params=pltpu.CompilerParams(
            dimension_semantics=("parallel", "arbitrary"), vmem_limit_bytes=VMEM_LIMIT),
        name="mlstm",
    )(proj, proj, proj, proj, gates, conv_w, conv_b, ln_g)


def _fox_bias_features(gt_ref, qf_scr, ka_scr, seq):
    tril = _tril(CS_BLK).astype(BF16)
    r = lax.broadcasted_iota(jnp.int32, (GATE_COLS, 2 * LANES), 0)
    c = lax.broadcasted_iota(jnp.int32, (GATE_COLS, 2 * LANES), 1)
    head = r - FX_F0
    is_head = (head >= 0) & (head < FOX_HEADS)
    sel = []
    for piece in range(3):
        q_hit = is_head & (c == FOX_FEAT * head + piece)
        k_hit = is_head & (c == LANES + FOX_FEAT * head + 3 + piece)
        sel.append(jnp.where(q_hit, 1.0, jnp.where(k_hit, -1.0, 0.0)).astype(BF16))
    lane = lax.broadcasted_iota(jnp.int32, (1, 2 * LANES), 1)
    slot = lane % FOX_FEAT
    in_use = (lane % LANES) < FOX_FEAT * FOX_HEADS
    q_one = (lane < LANES) & (slot >= 3) & (slot < 6)
    k_one = (lane >= LANES) & (slot < 3)
    ones = jnp.where(in_use & (q_one | k_one), 1.0, 0.0)

    carry = jnp.zeros((1, GATE_COLS), F32)
    for blk in range(seq // CS_BLK):
        rows = slice(blk * CS_BLK, (blk + 1) * CS_BLK)
        cs = _cumsum_rows(_log_sigmoid(gt_ref[rows, :]), tril) + carry
        carry = cs[CS_BLK - 1:CS_BLK, :]
        cs2 = cs * LOG2E
        hi = cs2.astype(BF16)
        r1 = cs2 - hi.astype(F32)
        mid = r1.astype(BF16)
        lo = (r1 - mid.astype(F32)).astype(BF16)
        feat = (_dot(hi, sel[0]) + _dot(mid, sel[1]) + _dot(lo, sel[2]) + ones).astype(BF16)
        qf_scr[rows, :] = feat[:, 0:LANES]
        ka_scr[rows, LANES:2 * LANES] = feat[:, LANES:2 * LANES]


def _fox_kernel(q_ref, k_ref, v_ref, gb_ref, gt_ref, z_ref, qf_scr, ka_scr, qt_scr, vt_scr, tri_scr, *, seq):
    tq = FOX_TQ
    hd = FOX_HEAD_DIM
    pair = pl.program_id(1)

    @pl.when(pair == 0)
    def _():
        _fox_bias_features(gt_ref, qf_scr, ka_scr, seq)
        key = lax.broadcasted_iota(jnp.int32, (tq, tq), 0)
        qry = lax.broadcasted_iota(jnp.int32, (tq, tq), 1)
        tri_scr[...] = jnp.where(qry >= key, 0.0, -jnp.inf)
        one_row = lax.broadcasted_iota(jnp.int32, (FOX_VROWS - hd, seq), 0) == 0
        for e in range(2):
            vt_scr[e, hd:FOX_VROWS, :] = jnp.where(one_row, 1.0, 0.0).astype(BF16)

    ka_scr[:, 0:LANES] = k_ref[...]
    sub = lax.broadcasted_iota(jnp.int32, (2 * LANES, 1), 0)
    keep = []
    for e in range(2):
        feat0 = LANES + FOX_FEAT * (2 * pair + e)
        own_q = (sub >= e * FOX_HEAD_DIM) & (sub < (e + 1) * FOX_HEAD_DIM)
        own_feat = (sub >= feat0) & (sub < feat0 + FOX_FEAT)
        keep.append(jnp.where(own_q | own_feat, 1.0, 0.0))
    for blk in range(seq // tq):
        rows = slice(blk * tq, (blk + 1) * tq)
        q_aug = jnp.concatenate([q_ref[rows, :].astype(F32), qf_scr[rows, :].astype(F32)], axis=1)
        q_aug_t = q_aug.T
        v_t = v_ref[rows, :].astype(F32).T.astype(BF16)
        for e in range(2):
            qt_scr[e, :, rows] = (q_aug_t * keep[e]).astype(BF16)
            vt_scr[e, 0:hd, rows] = v_t[e * hd:(e + 1) * hd, :]

    items = []
    for qi in range(seq // tq):
        q0 = qi * tq
        if FOX_TK <= tq:
            spans = [(k0, FOX_TK) for k0 in range(0, q0 + tq, FOX_TK)]
        else:
            spans = [(k0, FOX_TK) for k0 in range(0, q0 - FOX_TK + 1, FOX_TK)]
            k_done = len(spans) * FOX_TK
            spans.append((k_done, q0 + tq - k_done))
        for si, (k0, width) in enumerate(spans):
            for e in range(2):
                items.append((q0, k0, width, k0 + width > q0, e, si == 0, si == len(spans) - 1))
    n_items = len(items)

    def logits_of(idx):
        q0, k0, width, _, e, _, _ = items[idx]
        return _dot(ka_scr[k0:k0 + width, :], qt_scr[e, :, q0:q0 + tq])

    m = {}

    def softmax_of(idx, logit_t):
        q0, k0, width, masked, e, first, _ = items[idx]
        if masked:
            lo = max(q0 - k0, 0)
            diag = logit_t[lo:] + tri_scr[k0 + lo - q0:k0 + width - q0, :]
            logit_t = diag if lo == 0 else jnp.concatenate([logit_t[:lo], diag], axis=0)
        m_blk = jnp.max(logit_t, axis=0, keepdims=True)
        m_new = m_blk if first else jnp.maximum(m[e], m_blk)
        alpha = None if first else jnp.exp2(m[e] - m_new)
        m[e] = m_new
        return alpha, jnp.exp2(logit_t - m_new).astype(BF16)

    acc = {}
    logits = {i: logits_of(i) for i in range(min(2, n_items))}
    probs = {0: softmax_of(0, logits.pop(0))}
    for idx, (q0, k0, width, masked, e, first, last) in enumerate(items):
        if idx + 2 < n_items:
            logits[idx + 2] = logits_of(idx + 2)
        if idx + 1 < n_items:
            probs[idx + 1] = softmax_of(idx + 1, logits.pop(idx + 1))
        alpha, p_t = probs.pop(idx)
        pv = _dot(vt_scr[e, :, k0:k0 + width], p_t)
        acc[e] = pv if first else alpha * acc[e] + pv
        if last and e == 1:
            o_t = jnp.concatenate([acc[h][0:hd] / acc[h][hd:hd + 1] for h in range(2)], axis=0)
            z_ref[q0:q0 + tq, :] = (_sigmoid(gb_ref[q0:q0 + tq, :].astype(F32)) * o_t.T).astype(BF16)


def _fox_call(proj, gates, batch, seq):
    n = batch * seq
    pairs = FOX_HEADS // 2
    per_group = D_MODEL // LANES

    def cols(group):
        return pl.BlockSpec((seq, LANES), lambda b, p: (b, group * per_group + p))

    return pl.pallas_call(
        functools.partial(_fox_kernel, seq=seq),
        grid=(batch, pairs),
        in_specs=[
            cols(COL_FX_Q),
            cols(COL_FX_K),
            cols(COL_FX_V),
            cols(COL_G_B),
            pl.BlockSpec((seq, GATE_COLS), lambda b, p: (b, 0)),
        ],
        out_specs=pl.BlockSpec((seq, LANES), lambda b, p: (b, p)),
        out_shape=jax.ShapeDtypeStruct((n, D_MODEL), BF16),
        scratch_shapes=[
            pltpu.VMEM((seq, LANES), BF16),
            pltpu.VMEM((seq, 2 * LANES), BF16),
            pltpu.VMEM((2, 2 * LANES, seq), BF16),
            pltpu.VMEM((2, FOX_VROWS, seq), BF16),
            pltpu.VMEM((FOX_TQ, FOX_TQ), F32),
        ],
        compiler_params=pltpu.CompilerParams(
            dimension_semantics=("parallel", "arbitrary"), vmem_limit_bytes=VMEM_LIMIT),
        name="fox",
    )(proj, proj, proj, proj, gates)


def _layer(x2d, batch, seq, ffn1_norm, ffn1_w_gate, ffn1_w_up, ffn1_w_down, mix_norm, w_in, b_in,
           conv_w, conv_b, ml_head_norm, w_out, ffn2_norm, ffn2_w_gate, ffn2_w_up, ffn2_w_down, final_g):
    d = D_MODEL
    x1, h1 = _ffn_call(x2d, ffn1_norm[None, :], ffn1_w_gate.astype(BF16), ffn1_w_up.astype(BF16),
                       ffn1_w_down.astype(BF16), next_g=mix_norm[None, :])

    sizes = (d, d, d, d, ML_HEADS, ML_HEADS, d, d, d, FOX_HEADS, d, d)
    starts = [0]
    for s in sizes[:-1]:
        starts.append(starts[-1] + s)
    (ml_q, ml_k, ml_v, ml_o, ml_i, ml_f, fx_q, fx_k, fx_v, fx_f, g_a, g_b) = [
        slice(a, a + s) for a, s in zip(starts, sizes)]
    order = [None] * N_COL_GROUPS
    for col, grp in ((COL_ML_Q, ml_q), (COL_ML_K, ml_k), (COL_ML_V, ml_v), (COL_ML_O, ml_o), (COL_G_A, g_a),
                     (COL_G_B, g_b), (COL_FX_Q, fx_q), (COL_FX_K, fx_k), (COL_FX_V, fx_v)):
        order[col] = grp
    w_t = w_in.T
    w_main_t = jnp.concatenate([w_t[c] for c in order], axis=0).astype(BF16)
    b_main = jnp.concatenate([b_in[c] for c in order])[None, :]
    col_scale = jnp.ones((N_COL_GROUPS, d), F32).at[COL_FX_Q].set(LOG2E * FOX_HEAD_DIM ** -0.5).reshape(1, -1)
    pad = GATE_COLS - 2 * ML_HEADS - FOX_HEADS
    w_gates_t = jnp.concatenate([w_t[ml_i], w_t[ml_f], w_t[fx_f],
                                 jnp.zeros((pad, d), w_in.dtype)], axis=0).astype(BF16)
    b_gates = jnp.concatenate([b_in[ml_i], b_in[ml_f], b_in[fx_f], jnp.zeros((pad,), b_in.dtype)])[None, :]
    proj, gates = _proj_call(h1, w_main_t, b_main, col_scale, w_gates_t, b_gates)

    z_a = _mlstm_call(proj, gates, conv_w, conv_b[None, :], ml_head_norm[None, :], batch, seq)
    z_b = _fox_call(proj, gates, batch, seq)

    return _ffn_call(x1, ffn2_norm[None, :], ffn2_w_gate.astype(BF16), ffn2_w_up.astype(BF16),
                     ffn2_w_down.astype(BF16), pre=(z_a, z_b, w_out.astype(BF16)), final_g=final_g)


def kernel(x, ffn1_norm, ffn1_w_gate, ffn1_w_up, ffn1_w_down, mix_norm, w_in, b_in, conv_w, conv_b,
           ml_head_norm, w_out, ffn2_norm, ffn2_w_gate, ffn2_w_up, ffn2_w_down, final_norm):
    batch, seq, d = x.shape
    assert d == D_MODEL and ffn1_norm.shape[0] == 1, "single-layer configuration"
    out = _layer(x.reshape(batch * seq, d), batch, seq, ffn1_norm[0], ffn1_w_gate[0], ffn1_w_up[0],
                 ffn1_w_down[0], mix_norm[0], w_in[0], b_in[0], conv_w[0], conv_b[0], ml_head_norm[0],
                 w_out[0], ffn2_norm[0], ffn2_w_gate[0], ffn2_w_up[0], ffn2_w_down[0], final_norm[None, :])
    return out.reshape(batch, seq, d)
```

```python
import functools

import jax
import jax.numpy as jnp
from jax import lax
from jax.experimental import pallas as pl
from jax.experimental.pallas import tpu as pltpu

D_MODEL = 1024
ML_HEADS = 4
ML_HEAD_DIM = 256
CONV_WIDTH = 4
FOX_HEADS = 16
FOX_HEAD_DIM = 64
D_FF = 2816
EPS = 1e-6

LANES = 128
SUBLANES = 8
GATE_COLS = LANES
ML_I0, ML_F0, FX_F0 = 0, ML_HEADS, 2 * ML_HEADS

(COL_ML_Q, COL_ML_K, COL_ML_V, COL_ML_O, COL_G_A, COL_G_B, COL_FX_Q, COL_FX_K, COL_FX_V) = range(9)
N_COL_GROUPS = 9
LOG2E = 1.4426950408889634

FFN_TM = 512
FFN_TF = 256
PROJ_TM = 1024
PROJ_TN = 2304
ML_CHUNK = 256
HALO = 16
ML_PAD = 16
FOX_TQ = 256
FOX_TK = 2048
FOX_FEAT = 8
FOX_VROWS = 80
CS_BLK = 256
VMEM_LIMIT = 56 * 1024 * 1024

F32 = jnp.float32
BF16 = jnp.bfloat16


def _dot(a, b):
    return jnp.dot(a, b, preferred_element_type=F32)


def _dot_nt(a, b):
    return lax.dot_general(a, b, (((1,), (1,)), ((), ())), preferred_element_type=F32)


def _rmsnorm(x, g):
    return x * lax.rsqrt(jnp.mean(x * x, axis=-1, keepdims=True) + EPS) * g


def _sigmoid(x):
    return 1.0 / (1.0 + jnp.exp2(x * (-LOG2E)))


def _log_sigmoid(x):
    return jnp.minimum(x, 0.0) - jnp.log(1.0 + jnp.exp(-jnp.abs(x)))


def _cumsum_rows(x, tril):
    hi = x.astype(BF16)
    r1 = x - hi.astype(F32)
    mid = r1.astype(BF16)
    lo = (r1 - mid.astype(F32)).astype(BF16)
    return _dot(tril, hi) + _dot(tril, mid) + _dot(tril, lo)


def _tril(n):
    r = lax.broadcasted_iota(jnp.int32, (n, n), 0)
    c = lax.broadcasted_iota(jnp.int32, (n, n), 1)
    return r >= c


def _ffn_kernel(*refs, pre_proj, final_norm, next_norm):
    refs = list(refs)
    x_ref = refs.pop(0)
    if pre_proj:
        za_ref, zb_ref, wo_ref = refs.pop(0), refs.pop(0), refs.pop(0)
    g_ref, wg_ref, wu_ref, wd_ref = refs.pop(0), refs.pop(0), refs.pop(0), refs.pop(0)
    if final_norm:
        gf_ref = refs.pop(0)
    if next_norm:
        gn_ref = refs.pop(0)
    o_ref = refs.pop(0)
    if next_norm:
        hn_ref = refs.pop(0)
    (a_scr,) = refs

    x = x_ref[...]
    if pre_proj:
        y = (za_ref[...].astype(F32) + zb_ref[...].astype(F32)).astype(BF16)
        x = x + _dot(y, wo_ref[...])
    h = _rmsnorm(x, g_ref[...]).astype(BF16)
    for j in range(D_FF // FFN_TF):
        cols = slice(j * FFN_TF, (j + 1) * FFN_TF)
        gate = _dot(h, wg_ref[:, cols])
        up = _dot(h, wu_ref[:, cols])
        a_scr[:, cols] = (gate * _sigmoid(gate) * up).astype(BF16)
    out = x + 0.5 * _dot(a_scr[...], wd_ref[...])
    if next_norm:
        hn_ref[...] = _rmsnorm(out, gn_ref[...]).astype(BF16)
    if final_norm:
        out = _rmsnorm(out, gf_ref[...])
    o_ref[...] = out


def _ffn_call(x, norm_g, wg, wu, wd, pre=None, final_g=None, next_g=None):
    n = x.shape[0]
    row = pl.BlockSpec((FFN_TM, D_MODEL), lambda i: (i, 0))
    vec = pl.BlockSpec((1, D_MODEL), lambda i: (0, 0))
    out_specs, out_shape = row, jax.ShapeDtypeStruct((n, D_MODEL), F32)
    if next_g is not None:
        out_specs, out_shape = [row, row], [out_shape, jax.ShapeDtypeStruct((n, D_MODEL), BF16)]

    def resident(shape):
        return pl.BlockSpec(shape, lambda i: (0, 0), pipeline_mode=pl.Buffered(1))

    args, specs = [x], [row]
    if pre is not None:
        za, zb, wo = pre
        args += [za, zb, wo]
        specs += [row, row, resident((D_MODEL, D_MODEL))]
    args += [norm_g, wg, wu, wd]
    specs += [vec, resident((D_MODEL, D_FF)), resident((D_MODEL, D_FF)), resident((D_FF, D_MODEL))]
    for g in (final_g, next_g):
        if g is not None:
            args.append(g)
            specs.append(vec)
    return pl.pallas_call(
        functools.partial(_ffn_kernel, pre_proj=pre is not None, final_norm=final_g is not None,
                          next_norm=next_g is not None),
        grid=(n // FFN_TM,),
        in_specs=specs,
        out_specs=out_specs,
        out_shape=out_shape,
        scratch_shapes=[pltpu.VMEM((FFN_TM, D_FF), BF16)],
        compiler_params=pltpu.CompilerParams(
            dimension_semantics=("parallel",), vmem_limit_bytes=VMEM_LIMIT),
        name="ffn2" if pre is not None else "ffn1",
    )(*args)


def _pack_w_kernel(starts_ref, w_ref, o_ref):
    del starts_ref
    o_ref[...] = w_ref[...].astype(BF16)


def _pack_w_call(w_t, starts):
    n_groups = starts.shape[0]
    return pl.pallas_call(
        _pack_w_kernel,
        grid_spec=pltpu.PrefetchScalarGridSpec(
            num_scalar_prefetch=1,
            grid=(n_groups,),
            in_specs=[pl.BlockSpec((pl.Element(D_MODEL), pl.Element(D_MODEL)), lambda g, s: (s[g] * SUBLANES, 0))],
            out_specs=pl.BlockSpec((D_MODEL, D_MODEL), lambda g, s: (g, 0)),
        ),
        out_shape=jax.ShapeDtypeStruct((n_groups * D_MODEL, D_MODEL), BF16),
        compiler_params=pltpu.CompilerParams(dimension_semantics=("parallel",)),
        name="pack_w_in",
    )(starts, w_t)


def _proj_kernel(h_ref, w_ref, b_ref, s_ref, wgt_ref, bgt_ref, p_ref, gt_ref):
    h = h_ref[...]

    @pl.when(pl.program_id(1) == 0)
    def _():
        gt_ref[...] = _dot_nt(h, wgt_ref[...].astype(BF16)) + bgt_ref[...]

    p_ref[...] = ((_dot_nt(h, w_ref[...]) + b_ref[...]) * s_ref[...]).astype(BF16)


def _proj_call(h, w_t, b, col_scale, w_gates_t, b_gates):
    n = h.shape[0]
    n_out = w_t.shape[0]
    return pl.pallas_call(
        _proj_kernel,
        grid=(n // PROJ_TM, n_out // PROJ_TN),
        in_specs=[
            pl.BlockSpec((PROJ_TM, D_MODEL), lambda i, j: (i, 0)),
            pl.BlockSpec((PROJ_TN, D_MODEL), lambda i, j: (j, 0)),
            pl.BlockSpec((1, PROJ_TN), lambda i, j: (0, j)),
            pl.BlockSpec((1, PROJ_TN), lambda i, j: (0, j)),
            pl.BlockSpec((GATE_COLS, D_MODEL), lambda i, j: (0, 0)),
            pl.BlockSpec((1, GATE_COLS), lambda i, j: (0, 0)),
        ],
        out_specs=[
            pl.BlockSpec((PROJ_TM, PROJ_TN), lambda i, j: (i, j)),
            pl.BlockSpec((PROJ_TM, GATE_COLS), lambda i, j: (i, 0)),
        ],
        out_shape=[
            jax.ShapeDtypeStruct((n, n_out), BF16),
            jax.ShapeDtypeStruct((n, GATE_COLS), F32),
        ],
        compiler_params=pltpu.CompilerParams(
            dimension_semantics=("parallel", "arbitrary"), vmem_limit_bytes=VMEM_LIMIT),
        name="in_proj",
    )(h, w_t, b, col_scale, w_gates_t, b_gates)


def _mlstm_kernel(qk_ref, o_ref, ga_ref, v_ref, gt_ref, cw_ref, cb_ref, ln_ref, z_ref,
                  halo_scr, c_scr, m_scr):
    L = ML_CHUNK
    dh = ML_HEAD_DIM

    @pl.when(pl.program_id(1) == 0)
    def _():
        halo_scr[...] = jnp.zeros_like(halo_scr)
        c_scr[...] = jnp.zeros_like(c_scr)
        m_scr[...] = jnp.zeros_like(m_scr)

    u16 = qk_ref[...]
    halo16 = halo_scr[...]
    t_out = lax.broadcasted_iota(jnp.int32, (L, L), 0)
    t_in = lax.broadcasted_iota(jnp.int32, (L, L), 1)
    h_out = lax.broadcasted_iota(jnp.int32, (HALO, HALO), 0)
    h_in = lax.broadcasted_iota(jnp.int32, (HALO, HALO), 1)
    y = cb_ref[...] + cw_ref[CONV_WIDTH - 1:CONV_WIDTH, :] * u16.astype(F32)
    head_fix = jnp.zeros((HALO, 2 * D_MODEL), F32)
    for d in range(1, CONV_WIDTH):
        w_d = cw_ref[CONV_WIDTH - 1 - d:CONV_WIDTH - d, :]
        y = y + w_d * _dot(jnp.where(t_in == t_out - d, 1.0, 0.0).astype(BF16), u16)
        head_fix = head_fix + w_d * _dot(jnp.where(h_in == h_out - d + HALO, 1.0, 0.0).astype(BF16), halo16)
    y = jnp.concatenate([y[0:HALO] + head_fix, y[HALO:]], axis=0)
    halo_scr[...] = u16[L - HALO:L, :]
    qk = y * _sigmoid(y)

    gates = gt_ref[...]
    tril = _tril(L)
    bcum = _cumsum_rows(_log_sigmoid(gates), tril.astype(BF16))
    gates_t = gates.T
    bcum_t = bcum.T
    pad_row = lax.broadcasted_iota(jnp.int32, (ML_PAD, L), 0)
    ones_pad = jnp.where(pad_row == 0, 1.0, 0.0).astype(BF16)

    def matmul_stage(h):
        q_t = qk[:, h * dh:(h + 1) * dh].T.astype(BF16)
        k16 = (qk[:, D_MODEL + h * dh:D_MODEL + (h + 1) * dh] * (dh ** -0.5)).astype(BF16)
        state = c_scr[h]
        return q_t, k16, state, _dot(k16, q_t), _dot(state.astype(BF16), q_t)

    staged = matmul_stage(0)
    for h in range(ML_HEADS):
        cols = slice(h * dh, (h + 1) * dh)
        q_t, k16, state, qk_t, inter_t = staged
        if h + 1 < ML_HEADS:
            staged = matmul_stage(h + 1)
        v_aug = jnp.concatenate([v_ref[:, cols].astype(F32).T.astype(BF16), ones_pad], axis=0)
        li_row = gates_t[ML_I0 + h:ML_I0 + h + 1, :]
        b_row = bcum_t[ML_F0 + h:ML_F0 + h + 1, :]
        ib_col = gates[:, ML_I0 + h:ML_I0 + h + 1] - bcum[:, ML_F0 + h:ML_F0 + h + 1]
        m_prev = m_scr[h][:, 0:1]

        dmat_t = jnp.where(t_out <= t_in, b_row + ib_col, -jnp.inf)
        inter = b_row + m_prev
        m_t = jnp.maximum(jnp.max(dmat_t, axis=0, keepdims=True), inter)
        w_inter = jnp.exp(inter - m_t)
        s_t = (qk_t * jnp.exp(dmat_t - m_t)).astype(BF16)
        num_den = _dot(v_aug, s_t) + w_inter * inter_t
        den = num_den[dh:dh + 1, :]
        hh_t = num_den[0:dh, :] / jnp.maximum(jnp.abs(den), jnp.exp(-m_t))

        b_last = b_row[:, L - 1:L]
        log_w = b_last - b_row + li_row
        m_new = jnp.maximum(b_last + m_prev, jnp.max(log_w, axis=1, keepdims=True))
        decay = jnp.exp(b_last + m_prev - m_new)
        w = jnp.exp(log_w - m_new)
        c_scr[h] = decay * state + _dot((v_aug.astype(F32) * w).astype(BF16), k16)
        m_scr[h] = jnp.broadcast_to(m_new, (1, LANES))

        mu = jnp.mean(hh_t, axis=0, keepdims=True)
        dlt = hh_t - mu
        var = jnp.mean(dlt * dlt, axis=0, keepdims=True)
        hn = (dlt * lax.rsqrt(var + EPS)).T * ln_ref[:, cols]
        gate = _sigmoid(ga_ref[:, cols].astype(F32)) * _sigmoid(o_ref[:, cols].astype(F32))
        z_ref[:, cols] = (gate * hn).astype(BF16)


def _mlstm_call(proj, gates, conv_w, conv_b, ln_g, batch, seq):
    n = batch * seq
    nc = seq // ML_CHUNK
    L = ML_CHUNK

    def row(width, col):
        return pl.BlockSpec((L, width), lambda b, c: (b * nc + c, col))

    def full(shape):
        return pl.BlockSpec(shape, lambda b, c: (0, 0))

    return pl.pallas_call(
        _mlstm_kernel,
        grid=(batch, nc),
        in_specs=[
            row(2 * D_MODEL, COL_ML_Q // 2),
            row(D_MODEL, COL_ML_O),
            row(D_MODEL, COL_G_A),
            row(D_MODEL, COL_ML_V),
            row(GATE_COLS, 0),
            full((CONV_WIDTH, 2 * D_MODEL)),
            full((1, 2 * D_MODEL)),
            full((1, D_MODEL)),
        ],
        out_specs=row(D_MODEL, 0),
        out_shape=jax.ShapeDtypeStruct((n, D_MODEL), BF16),
        scratch_shapes=[
            pltpu.VMEM((HALO, 2 * D_MODEL), BF16),
            pltpu.VMEM((ML_HEADS, ML_HEAD_DIM + ML_PAD, ML_HEAD_DIM), F32),
            pltpu.VMEM((ML_HEADS, 1, LANES), F32),
        ],
        compiler_params=pltpu.CompilerParams(
            dimension_semantics=("parallel", "arbitrary"), vmem_limit_bytes=VMEM_LIMIT),
        name="mlstm",
    )(proj, proj, proj, proj, gates, conv_w, conv_b, ln_g)


def _fox_bias_features(gt_ref, qf_scr, ka_scr, seq):
    tril = _tril(CS_BLK).astype(BF16)
    r = lax.broadcasted_iota(jnp.int32, (GATE_COLS, 2 * LANES), 0)
    c = lax.broadcasted_iota(jnp.int32, (GATE_COLS, 2 * LANES), 1)
    head = r - FX_F0
    is_head = (head >= 0) & (head < FOX_HEADS)
    sel = []
    for piece in range(3):
        q_hit = is_head & (c == FOX_FEAT * head + piece)
        k_hit = is_head & (c == LANES + FOX_FEAT * head + 3 + piece)
        sel.append(jnp.where(q_hit, 1.0, jnp.where(k_hit, -1.0, 0.0)).astype(BF16))
    lane = lax.broadcasted_iota(jnp.int32, (1, 2 * LANES), 1)
    slot = lane % FOX_FEAT
    in_use = (lane % LANES) < FOX_FEAT * FOX_HEADS
    q_one = (lane < LANES) & (slot >= 3) & (slot < 6)
    k_one = (lane >= LANES) & (slot < 3)
    ones = jnp.where(in_use & (q_one | k_one), 1.0, 0.0)

    carry = jnp.zeros((1, GATE_COLS), F32)
    for blk in range(seq // CS_BLK):
        rows = slice(blk * CS_BLK, (blk + 1) * CS_BLK)
        cs = _cumsum_rows(_log_sigmoid(gt_ref[rows, :]), tril) + carry
        carry = cs[CS_BLK - 1:CS_BLK, :]
        cs2 = cs * LOG2E
        hi = cs2.astype(BF16)
        r1 = cs2 - hi.astype(F32)
        mid = r1.astype(BF16)
        lo = (r1 - mid.astype(F32)).astype(BF16)
        feat = (_dot(hi, sel[0]) + _dot(mid, sel[1]) + _dot(lo, sel[2]) + ones).astype(BF16)
        qf_scr[rows, :] = feat[:, 0:LANES]
        ka_scr[rows, LANES:2 * LANES] = feat[:, LANES:2 * LANES]


def _fox_kernel(q_ref, k_ref, v_ref, gb_ref, gt_ref, z_ref, qf_scr, ka_scr, qt_scr, vt_scr, tri_scr, *, seq):
    tq = FOX_TQ
    hd = FOX_HEAD_DIM
    pair = pl.program_id(1)

    @pl.when(pair == 0)
    def _():
        _fox_bias_features(gt_ref, qf_scr, ka_scr, seq)
        key = lax.broadcasted_iota(jnp.int32, (tq, tq), 0)
        qry = lax.broadcasted_iota(jnp.int32, (tq, tq), 1)
        tri_scr[...] = jnp.where(qry >= key, 0.0, -jnp.inf)
        one_row = lax.broadcasted_iota(jnp.int32, (FOX_VROWS - hd, seq), 0) == 0
        for e in range(2):
            vt_scr[e, hd:FOX_VROWS, :] = jnp.where(one_row, 1.0, 0.0).astype(BF16)

    ka_scr[:, 0:LANES] = k_ref[...]
    sub = lax.broadcasted_iota(jnp.int32, (2 * LANES, 1), 0)
    keep = []
    for e in range(2):
        feat0 = LANES + FOX_FEAT * (2 * pair + e)
        own_q = (sub >= e * FOX_HEAD_DIM) & (sub < (e + 1) * FOX_HEAD_DIM)
        own_feat = (sub >= feat0) & (sub < feat0 + FOX_FEAT)
        keep.append(jnp.where(own_q | own_feat, 1.0, 0.0))
    for blk in range(seq // tq):
        rows = slice(blk * tq, (blk + 1) * tq)
        q_aug = jnp.concatenate([q_ref[rows, :].astype(F32), qf_scr[rows, :].astype(F32)], axis=1)
        q_aug_t = q_aug.T
        v_t = v_ref[rows, :].astype(F32).T.astype(BF16)
        for e in range(2):
            qt_scr[e, :, rows] = (q_aug_t * keep[e]).astype(BF16)
            vt_scr[e, 0:hd, rows] = v_t[e * hd:(e + 1) * hd, :]

    items = []
    for qi in range(seq // tq):
        q0 = qi * tq
        if FOX_TK <= tq:
            spans = [(k0, FOX_TK) for k0 in range(0, q0 + tq, FOX_TK)]
        else:
            spans = [(k0, FOX_TK) for k0 in range(0, q0 - FOX_TK + 1, FOX_TK)]
            k_done = len(spans) * FOX_TK
            spans.append((k_done, q0 + tq - k_done))
        for si, (k0, width) in enumerate(spans):
            for e in range(2):
                items.append((q0, k0, width, k0 + width > q0, e, si == 0, si == len(spans) - 1))
    n_items = len(items)

    def logits_of(idx):
        q0, k0, width, _, e, _, _ = items[idx]
        return _dot(ka_scr[k0:k0 + width, :], qt_scr[e, :, q0:q0 + tq])

    m = {}

    def softmax_of(idx, logit_t):
        q0, k0, width, masked, e, first, _ = items[idx]
        if masked:
            lo = max(q0 - k0, 0)
            diag = logit_t[lo:] + tri_scr[k0 + lo - q0:k0 + width - q0, :]
            logit_t = diag if lo == 0 else jnp.concatenate([logit_t[:lo], diag], axis=0)
        m_blk = jnp.max(logit_t, axis=0, keepdims=True)
        m_new = m_blk if first else jnp.maximum(m[e], m_blk)
        alpha = None if first else jnp.exp2(m[e] - m_new)
        m[e] = m_new
        return alpha, jnp.exp2(logit_t - m_new).astype(BF16)

    acc = {}
    logits = {i: logits_of(i) for i in range(min(2, n_items))}
    probs = {0: softmax_of(0, logits.pop(0))}
    for idx, (q0, k0, width, masked, e, first, last) in enumerate(items):
        if idx + 2 < n_items:
            logits[idx + 2] = logits_of(idx + 2)
        if idx + 1 < n_items:
            probs[idx + 1] = softmax_of(idx + 1, logits.pop(idx + 1))
        alpha, p_t = probs.pop(idx)
        pv = _dot(vt_scr[e, :, k0:k0 + width], p_t)
        acc[e] = pv if first else alpha * acc[e] + pv
        if last and e == 1:
            o_t = jnp.concatenate([acc[h][0:hd] / acc[h][hd:hd + 1] for h in range(2)], axis=0)
            z_ref[q0:q0 + tq, :] = (_sigmoid(gb_ref[q0:q0 + tq, :].astype(F32)) * o_t.T).astype(BF16)


def _fox_call(proj, gates, batch, seq):
    n = batch * seq
    pairs = FOX_HEADS // 2
    per_group = D_MODEL // LANES

    def cols(group):
        return pl.BlockSpec((seq, LANES), lambda b, p: (b, group * per_group + p))

    return pl.pallas_call(
        functools.partial(_fox_kernel, seq=seq),
        grid=(batch, pairs),
        in_specs=[
            cols(COL_FX_Q),
            cols(COL_FX_K),
            cols(COL_FX_V),
            cols(COL_G_B),
            pl.BlockSpec((seq, GATE_COLS), lambda b, p: (b, 0)),
        ],
        out_specs=pl.BlockSpec((seq, LANES), lambda b, p: (b, p)),
        out_shape=jax.ShapeDtypeStruct((n, D_MODEL), BF16),
        scratch_shapes=[
            pltpu.VMEM((seq, LANES), BF16),
            pltpu.VMEM((seq, 2 * LANES), BF16),
            pltpu.VMEM((2, 2 * LANES, seq), BF16),
            pltpu.VMEM((2, FOX_VROWS, seq), BF16),
            pltpu.VMEM((FOX_TQ, FOX_TQ), F32),
        ],
        compiler_params=pltpu.CompilerParams(
            dimension_semantics=("parallel", "arbitrary"), vmem_limit_bytes=VMEM_LIMIT),
        name="fox",
    )(proj, proj, proj, proj, gates)


def _layer(x2d, batch, seq, ffn1_norm, ffn1_w_gate, ffn1_w_up, ffn1_w_down, mix_norm, w_in, b_in,
           conv_w, conv_b, ml_head_norm, w_out, ffn2_norm, ffn2_w_gate, ffn2_w_up, ffn2_w_down, final_g):
    d = D_MODEL
    x1, h1 = _ffn_call(x2d, ffn1_norm[None, :], ffn1_w_gate.astype(BF16), ffn1_w_up.astype(BF16),
                       ffn1_w_down.astype(BF16), next_g=mix_norm[None, :])

    sizes = (d, d, d, d, ML_HEADS, ML_HEADS, d, d, d, FOX_HEADS, d, d)
    starts = [0]
    for s in sizes[:-1]:
        starts.append(starts[-1] + s)
    (ml_q, ml_k, ml_v, ml_o, ml_i, ml_f, fx_q, fx_k, fx_v, fx_f, g_a, g_b) = [
        slice(a, a + s) for a, s in zip(starts, sizes)]
    order = [None] * N_COL_GROUPS
    for col, grp in ((COL_ML_Q, ml_q), (COL_ML_K, ml_k), (COL_ML_V, ml_v), (COL_ML_O, ml_o), (COL_G_A, g_a),
                     (COL_G_B, g_b), (COL_FX_Q, fx_q), (COL_FX_K, fx_k), (COL_FX_V, fx_v)):
        order[col] = grp
    w_t = w_in.T
    assert all(c.start % SUBLANES == 0 for c in order)
    w_main_t = _pack_w_call(w_t, jnp.asarray([c.start // SUBLANES for c in order], jnp.int32))
    b_main = jnp.concatenate([b_in[c] for c in order])[None, :]
    col_scale = jnp.ones((N_COL_GROUPS, d), F32).at[COL_FX_Q].set(LOG2E * FOX_HEAD_DIM ** -0.5).reshape(1, -1)
    pad = GATE_COLS - 2 * ML_HEADS - FOX_HEADS
    w_gates_t = jnp.concatenate([w_t[ml_i], w_t[ml_f], w_t[fx_f], jnp.zeros((pad, d), w_in.dtype)], axis=0)
    b_gates = jnp.concatenate([b_in[ml_i], b_in[ml_f], b_in[fx_f], jnp.zeros((pad,), b_in.dtype)])[None, :]
    proj, gates = _proj_call(h1, w_main_t, b_main, col_scale, w_gates_t, b_gates)

    z_a = _mlstm_call(proj, gates, conv_w, conv_b[None, :], ml_head_norm[None, :], batch, seq)
    z_b = _fox_call(proj, gates, batch, seq)

    return _ffn_call(x1, ffn2_norm[None, :], ffn2_w_gate.astype(BF16), ffn2_w_up.astype(BF16),
                     ffn2_w_down.astype(BF16), pre=(z_a, z_b, w_out.astype(BF16)), final_g=final_g)


def kernel(x, ffn1_norm, ffn1_w_gate, ffn1_w_up, ffn1_w_down, mix_norm, w_in, b_in, conv_w, conv_b,
           ml_head_norm, w_out, ffn2_norm, ffn2_w_gate, ffn2_w_up, ffn2_w_down, final_norm):
    batch, seq, d = x.shape
    assert d == D_MODEL and ffn1_norm.shape[0] == 1, "single-layer configuration"
    out = _layer(x.reshape(batch * seq, d), batch, seq, ffn1_norm[0], ffn1_w_gate[0], ffn1_w_up[0],
                 ffn1_w_down[0], mix_norm[0], w_in[0], b_in[0], conv_w[0], conv_b[0], ml_head_norm[0],
                 w_out[0], ffn2_norm[0], ffn2_w_gate[0], ffn2_w_up[0], ffn2_w_down[0], final_norm[None, :])
    return out.reshape(batch, seq, d)
```

```python
import functools

import jax
import jax.numpy as jnp
from jax import lax
from jax.experimental import pallas as pl
from jax.experimental.pallas import tpu as pltpu

D_MODEL = 1024
ML_HEADS = 4
ML_HEAD_DIM = 256
CONV_WIDTH = 4
FOX_HEADS = 16
FOX_HEAD_DIM = 64
D_FF = 2816
EPS = 1e-6

LANES = 128
SUBLANES = 8
GATE_COLS = LANES
ML_I0, ML_F0, FX_F0 = 0, ML_HEADS, 2 * ML_HEADS

(COL_ML_Q, COL_ML_K, COL_ML_V, COL_ML_O, COL_G_A, COL_G_B, COL_FX_Q, COL_FX_K, COL_FX_V) = range(9)
N_COL_GROUPS = 9
LOG2E = 1.4426950408889634

FFN_TM = 1024
FFN_SUB = 512
FFN_TF = 256
PROJ_TM = 1024
PROJ_TN = 4608
ML_CHUNK = 256
HALO = 16
ML_PAD = 16
FOX_TQ = 256
FOX_TK = 2048
FOX_FEAT = 8
FOX_VROWS = 80
CS_BLK = 256
VMEM_LIMIT = 56 * 1024 * 1024

F32 = jnp.float32
BF16 = jnp.bfloat16


def _dot(a, b):
    return jnp.dot(a, b, preferred_element_type=F32)


def _dot_nt(a, b):
    return lax.dot_general(a, b, (((1,), (1,)), ((), ())), preferred_element_type=F32)


def _rmsnorm(x, g):
    return x * lax.rsqrt(jnp.mean(x * x, axis=-1, keepdims=True) + EPS) * g


def _sigmoid(x):
    return 1.0 / (1.0 + jnp.exp2(x * (-LOG2E)))


def _log_sigmoid(x):
    return jnp.minimum(x, 0.0) - jnp.log(1.0 + jnp.exp(-jnp.abs(x)))


def _cumsum_rows(x, tril):
    hi = x.astype(BF16)
    r1 = x - hi.astype(F32)
    mid = r1.astype(BF16)
    lo = (r1 - mid.astype(F32)).astype(BF16)
    return _dot(tril, hi) + _dot(tril, mid) + _dot(tril, lo)


def _tril(n):
    r = lax.broadcasted_iota(jnp.int32, (n, n), 0)
    c = lax.broadcasted_iota(jnp.int32, (n, n), 1)
    return r >= c


def _ffn_kernel(*refs, pre_proj, final_norm, next_norm):
    refs = list(refs)
    x_ref = refs.pop(0)
    if pre_proj:
        za_ref, zb_ref, wo_ref = refs.pop(0), refs.pop(0), refs.pop(0)
    g_ref, wg_ref, wu_ref, wd_ref = refs.pop(0), refs.pop(0), refs.pop(0), refs.pop(0)
    if final_norm:
        gf_ref = refs.pop(0)
    if next_norm:
        gn_ref = refs.pop(0)
    o_ref = refs.pop(0)
    if next_norm:
        hn_ref = refs.pop(0)
    (a_scr,) = refs

    def prologue(s):
        rows = slice(s * FFN_SUB, (s + 1) * FFN_SUB)
        x = x_ref[rows, :]
        if pre_proj:
            y = (za_ref[rows, :].astype(F32) + zb_ref[rows, :].astype(F32)).astype(BF16)
            x = x + _dot(y, wo_ref[...])
        return x, _rmsnorm(x, g_ref[...]).astype(BF16)

    def epilogue(s, x, y):
        rows = slice(s * FFN_SUB, (s + 1) * FFN_SUB)
        out = x + 0.5 * y
        if next_norm:
            hn_ref[rows, :] = _rmsnorm(out, gn_ref[...]).astype(BF16)
        if final_norm:
            out = _rmsnorm(out, gf_ref[...])
        o_ref[rows, :] = out

    n_sub = FFN_TM // FFN_SUB
    x, h = prologue(0)
    prev = None
    for s in range(n_sub):
        nxt = None
        for j in range(D_FF // FFN_TF):
            cols = slice(j * FFN_TF, (j + 1) * FFN_TF)
            gate = _dot(h, wg_ref[:, cols])
            up = _dot(h, wu_ref[:, cols])
            a_scr[s % 2, :, cols] = (gate * _sigmoid(gate) * up).astype(BF16)
            if j == 0 and prev is not None:
                prev = prev + (_dot(a_scr[(s - 1) % 2], wd_ref[...]),)
            if j == 1 and s + 1 < n_sub:
                nxt = prologue(s + 1)
            if j == 3 and prev is not None:
                epilogue(*prev)
                prev = None
        prev = (s, x)
        if nxt is not None:
            x, h = nxt
    epilogue(*prev, _dot(a_scr[(n_sub - 1) % 2], wd_ref[...]))


def _ffn_call(x, norm_g, wg, wu, wd, pre=None, final_g=None, next_g=None):
    n = x.shape[0]
    row = pl.BlockSpec((FFN_TM, D_MODEL), lambda i: (i, 0))
    vec = pl.BlockSpec((1, D_MODEL), lambda i: (0, 0))
    out_specs, out_shape = row, jax.ShapeDtypeStruct((n, D_MODEL), F32)
    if next_g is not None:
        out_specs, out_shape = [row, row], [out_shape, jax.ShapeDtypeStruct((n, D_MODEL), BF16)]

    def resident(shape):
        return pl.BlockSpec(shape, lambda i: (0, 0), pipeline_mode=pl.Buffered(1))

    args, specs = [x], [row]
    if pre is not None:
        za, zb, wo = pre
        args += [za, zb, wo]
        specs += [row, row, resident((D_MODEL, D_MODEL))]
    args += [norm_g, wg, wu, wd]
    specs += [vec, resident((D_MODEL, D_FF)), resident((D_MODEL, D_FF)), resident((D_FF, D_MODEL))]
    for g in (final_g, next_g):
        if g is not None:
            args.append(g)
            specs.append(vec)
    return pl.pallas_call(
        functools.partial(_ffn_kernel, pre_proj=pre is not None, final_norm=final_g is not None,
                          next_norm=next_g is not None),
        grid=(n // FFN_TM,),
        in_specs=specs,
        out_specs=out_specs,
        out_shape=out_shape,
        scratch_shapes=[pltpu.VMEM((2, FFN_SUB, D_FF), BF16)],
        compiler_params=pltpu.CompilerParams(
            dimension_semantics=("parallel",), vmem_limit_bytes=VMEM_LIMIT),
        name="ffn2" if pre is not None else "ffn1",
    )(*args)


def _pack_w_kernel(starts_ref, w_ref, o_ref):
    del starts_ref
    o_ref[...] = w_ref[...].astype(BF16)


def _pack_w_call(w_t, starts):
    n_groups = starts.shape[0]
    return pl.pallas_call(
        _pack_w_kernel,
        grid_spec=pltpu.PrefetchScalarGridSpec(
            num_scalar_prefetch=1,
            grid=(n_groups,),
            in_specs=[pl.BlockSpec((pl.Element(D_MODEL), pl.Element(D_MODEL)), lambda g, s: (s[g] * SUBLANES, 0))],
            out_specs=pl.BlockSpec((D_MODEL, D_MODEL), lambda g, s: (g, 0)),
        ),
        out_shape=jax.ShapeDtypeStruct((n_groups * D_MODEL, D_MODEL), BF16),
        compiler_params=pltpu.CompilerParams(dimension_semantics=("parallel",)),
        name="pack_w_in",
    )(starts, w_t)


def _proj_kernel(h_ref, w_ref, b_ref, s_ref, wgt_ref, bgt_ref, p_ref, gt_ref):
    h = h_ref[...]

    @pl.when(pl.program_id(1) == 0)
    def _():
        gt_ref[...] = _dot_nt(h, wgt_ref[...].astype(BF16)) + bgt_ref[...]

    p_ref[...] = ((_dot_nt(h, w_ref[...]) + b_ref[...]) * s_ref[...]).astype(BF16)


def _proj_call(h, w_t, b, col_scale, w_gates_t, b_gates):
    n = h.shape[0]
    n_out = w_t.shape[0]
    return pl.pallas_call(
        _proj_kernel,
        grid=(n // PROJ_TM, n_out // PROJ_TN),
        in_specs=[
            pl.BlockSpec((PROJ_TM, D_MODEL), lambda i, j: (i, 0)),
            pl.BlockSpec((PROJ_TN, D_MODEL), lambda i, j: (j, 0)),
            pl.BlockSpec((1, PROJ_TN), lambda i, j: (0, j)),
            pl.BlockSpec((1, PROJ_TN), lambda i, j: (0, j)),
            pl.BlockSpec((GATE_COLS, D_MODEL), lambda i, j: (0, 0)),
            pl.BlockSpec((1, GATE_COLS), lambda i, j: (0, 0)),
        ],
        out_specs=[
            pl.BlockSpec((PROJ_TM, PROJ_TN), lambda i, j: (i, j)),
            pl.BlockSpec((PROJ_TM, GATE_COLS), lambda i, j: (i, 0)),
        ],
        out_shape=[
            jax.ShapeDtypeStruct((n, n_out), BF16),
            jax.ShapeDtypeStruct((n, GATE_COLS), F32),
        ],
        compiler_params=pltpu.CompilerParams(
            dimension_semantics=("parallel", "arbitrary"), vmem_limit_bytes=VMEM_LIMIT),
        name="in_proj",
    )(h, w_t, b, col_scale, w_gates_t, b_gates)


def _mlstm_kernel(qk_ref, o_ref, ga_ref, v_ref, gt_ref, cw_ref, cb_ref, ln_ref, z_ref,
                  halo_scr, c_scr, m_scr):
    L = ML_CHUNK
    dh = ML_HEAD_DIM

    @pl.when(pl.program_id(1) == 0)
    def _():
        halo_scr[...] = jnp.zeros_like(halo_scr)
        c_scr[...] = jnp.zeros_like(c_scr)
        m_scr[...] = jnp.zeros_like(m_scr)

    u16 = qk_ref[...]
    halo16 = halo_scr[...]
    t_out = lax.broadcasted_iota(jnp.int32, (L, L), 0)
    t_in = lax.broadcasted_iota(jnp.int32, (L, L), 1)
    h_out = lax.broadcasted_iota(jnp.int32, (HALO, HALO), 0)
    h_in = lax.broadcasted_iota(jnp.int32, (HALO, HALO), 1)
    y = cb_ref[...] + cw_ref[CONV_WIDTH - 1:CONV_WIDTH, :] * u16.astype(F32)
    head_fix = jnp.zeros((HALO, 2 * D_MODEL), F32)
    for d in range(1, CONV_WIDTH):
        w_d = cw_ref[CONV_WIDTH - 1 - d:CONV_WIDTH - d, :]
        y = y + w_d * _dot(jnp.where(t_in == t_out - d, 1.0, 0.0).astype(BF16), u16)
        head_fix = head_fix + w_d * _dot(jnp.where(h_in == h_out - d + HALO, 1.0, 0.0).astype(BF16), halo16)
    y = jnp.concatenate([y[0:HALO] + head_fix, y[HALO:]], axis=0)
    halo_scr[...] = u16[L - HALO:L, :]
    qk = y * _sigmoid(y)

    gates = gt_ref[...]
    tril = _tril(L)
    bcum = _cumsum_rows(_log_sigmoid(gates), tril.astype(BF16))
    gates_t = gates.T
    bcum_t = bcum.T
    pad_row = lax.broadcasted_iota(jnp.int32, (ML_PAD, L), 0)
    ones_pad = jnp.where(pad_row == 0, 1.0, 0.0).astype(BF16)

    def matmul_stage(h):
        q_t = qk[:, h * dh:(h + 1) * dh].T.astype(BF16)
        k16 = (qk[:, D_MODEL + h * dh:D_MODEL + (h + 1) * dh] * (dh ** -0.5)).astype(BF16)
        state = c_scr[h]
        return q_t, k16, state, _dot(k16, q_t), _dot(state.astype(BF16), q_t)

    staged = matmul_stage(0)
    for h in range(ML_HEADS):
        cols = slice(h * dh, (h + 1) * dh)
        q_t, k16, state, qk_t, inter_t = staged
        if h + 1 < ML_HEADS:
            staged = matmul_stage(h + 1)
        v_aug = jnp.concatenate([v_ref[:, cols].astype(F32).T.astype(BF16), ones_pad], axis=0)
        li_row = gates_t[ML_I0 + h:ML_I0 + h + 1, :]
        b_row = bcum_t[ML_F0 + h:ML_F0 + h + 1, :]
        ib_col = gates[:, ML_I0 + h:ML_I0 + h + 1] - bcum[:, ML_F0 + h:ML_F0 + h + 1]
        m_prev = m_scr[h][:, 0:1]

        dmat_t = jnp.where(t_out <= t_in, b_row + ib_col, -jnp.inf)
        inter = b_row + m_prev
        m_t = jnp.maximum(jnp.max(dmat_t, axis=0, keepdims=True), inter)
        w_inter = jnp.exp(inter - m_t)
        s_t = (qk_t * jnp.exp(dmat_t - m_t)).astype(BF16)
        num_den = _dot(v_aug, s_t) + w_inter * inter_t
        den = num_den[dh:dh + 1, :]
        hh_t = num_den[0:dh, :] / jnp.maximum(jnp.abs(den), jnp.exp(-m_t))

        b_last = b_row[:, L - 1:L]
        log_w = b_last - b_row + li_row
        m_new = jnp.maximum(b_last + m_prev, jnp.max(log_w, axis=1, keepdims=True))
        decay = jnp.exp(b_last + m_prev - m_new)
        w = jnp.exp(log_w - m_new)
        c_scr[h] = decay * state + _dot((v_aug.astype(F32) * w).astype(BF16), k16)
        m_scr[h] = jnp.broadcast_to(m_new, (1, LANES))

        mu = jnp.mean(hh_t, axis=0, keepdims=True)
        dlt = hh_t - mu
        var = jnp.mean(dlt * dlt, axis=0, keepdims=True)
        hn = (dlt * lax.rsqrt(var + EPS)).T * ln_ref[:, cols]
        gate = _sigmoid(ga_ref[:, cols].astype(F32)) * _sigmoid(o_ref[:, cols].astype(F32))
        z_ref[:, cols] = (gate * hn).astype(BF16)


def _mlstm_call(proj, gates, conv_w, conv_b, ln_g, batch, seq):
    n = batch * seq
    nc = seq // ML_CHUNK
    L = ML_CHUNK

    def row(width, col):
        return pl.BlockSpec((L, width), lambda b, c: (b * nc + c, col))

    def full(shape):
        return pl.BlockSpec(shape, lambda b, c: (0, 0))

    return pl.pallas_call(
        _mlstm_kernel,
        grid=(batch, nc),
        in_specs=[
            row(2 * D_MODEL, COL_ML_Q // 2),
            row(D_MODEL, COL_ML_O),
            row(D_MODEL, COL_G_A),
            row(D_MODEL, COL_ML_V),
            row(GATE_COLS, 0),
            full((CONV_WIDTH, 2 * D_MODEL)),
            full((1, 2 * D_MODEL)),
            full((1, D_MODEL)),
        ],
        out_specs=row(D_MODEL, 0),
        out_shape=jax.ShapeDtypeStruct((n, D_MODEL), BF16),
        scratch_shapes=[
            pltpu.VMEM((HALO, 2 * D_MODEL), BF16),
            pltpu.VMEM((ML_HEADS, ML_HEAD_DIM + ML_PAD, ML_HEAD_DIM), F32),
            pltpu.VMEM((ML_HEADS, 1, LANES), F32),
        ],
        compiler_params=pltpu.CompilerParams(
            dimension_semantics=("parallel", "arbitrary"), vmem_limit_bytes=VMEM_LIMIT),
        name="mlstm",
    )(proj, proj, proj, proj, gates, conv_w, conv_b, ln_g)


def _fox_bias_features(gt_ref, qf_scr, ka_scr, seq):
    tril = _tril(CS_BLK).astype(BF16)
    r = lax.broadcasted_iota(jnp.int32, (GATE_COLS, 2 * LANES), 0)
    c = lax.broadcasted_iota(jnp.int32, (GATE_COLS, 2 * LANES), 1)
    head = r - FX_F0
    is_head = (head >= 0) & (head < FOX_HEADS)
    sel = []
    for piece in range(3):
        q_hit = is_head & (c == FOX_FEAT * head + piece)
        k_hit = is_head & (c == LANES + FOX_FEAT * head + 3 + piece)
        sel.append(jnp.where(q_hit, 1.0, jnp.where(k_hit, -1.0, 0.0)).astype(BF16))
    lane = lax.broadcasted_iota(jnp.int32, (1, 2 * LANES), 1)
    slot = lane % FOX_FEAT
    in_use = (lane % LANES) < FOX_FEAT * FOX_HEADS
    q_one = (lane < LANES) & (slot >= 3) & (slot < 6)
    k_one = (lane >= LANES) & (slot < 3)
    ones = jnp.where(in_use & (q_one | k_one), 1.0, 0.0)

    carry = jnp.zeros((1, GATE_COLS), F32)
    for blk in range(seq // CS_BLK):
        rows = slice(blk * CS_BLK, (blk + 1) * CS_BLK)
        cs = _cumsum_rows(_log_sigmoid(gt_ref[rows, :]), tril) + carry
        carry = cs[CS_BLK - 1:CS_BLK, :]
        cs2 = cs * LOG2E
        hi = cs2.astype(BF16)
        r1 = cs2 - hi.astype(F32)
        mid = r1.astype(BF16)
        lo = (r1 - mid.astype(F32)).astype(BF16)
        feat = (_dot(hi, sel[0]) + _dot(mid, sel[1]) + _dot(lo, sel[2]) + ones).astype(BF16)
        qf_scr[rows, :] = feat[:, 0:LANES]
        ka_scr[rows, LANES:2 * LANES] = feat[:, LANES:2 * LANES]


def _fox_kernel(q_ref, k_ref, v_ref, gb_ref, gt_ref, z_ref, qf_scr, ka_scr, qt_scr, vt_scr, tri_scr, *, seq):
    tq = FOX_TQ
    hd = FOX_HEAD_DIM
    pair = pl.program_id(1)

    @pl.when(pair == 0)
    def _():
        _fox_bias_features(gt_ref, qf_scr, ka_scr, seq)
        key = lax.broadcasted_iota(jnp.int32, (tq, tq), 0)
        qry = lax.broadcasted_iota(jnp.int32, (tq, tq), 1)
        tri_scr[...] = jnp.where(qry >= key, 0.0, -jnp.inf)
        one_row = lax.broadcasted_iota(jnp.int32, (FOX_VROWS - hd, seq), 0) == 0
        for e in range(2):
            vt_scr[e, hd:FOX_VROWS, :] = jnp.where(one_row, 1.0, 0.0).astype(BF16)

    ka_scr[:, 0:LANES] = k_ref[...]
    sub = lax.broadcasted_iota(jnp.int32, (2 * LANES, 1), 0)
    keep = []
    for e in range(2):
        feat0 = LANES + FOX_FEAT * (2 * pair + e)
        own_q = (sub >= e * FOX_HEAD_DIM) & (sub < (e + 1) * FOX_HEAD_DIM)
        own_feat = (sub >= feat0) & (sub < feat0 + FOX_FEAT)
        keep.append(jnp.where(own_q | own_feat, 1.0, 0.0))
    for blk in range(seq // tq):
        rows = slice(blk * tq, (blk + 1) * tq)
        q_aug = jnp.concatenate([q_ref[rows, :].astype(F32), qf_scr[rows, :].astype(F32)], axis=1)
        q_aug_t = q_aug.T
        v_t = v_ref[rows, :].astype(F32).T.astype(BF16)
        for e in range(2):
            qt_scr[e, :, rows] = (q_aug_t * keep[e]).astype(BF16)
            vt_scr[e, 0:hd, rows] = v_t[e * hd:(e + 1) * hd, :]

    items = []
    for qi in range(seq // tq):
        q0 = qi * tq
        if FOX_TK <= tq:
            spans = [(k0, FOX_TK) for k0 in range(0, q0 + tq, FOX_TK)]
        else:
            spans = [(k0, FOX_TK) for k0 in range(0, q0 - FOX_TK + 1, FOX_TK)]
            k_done = len(spans) * FOX_TK
            spans.append((k_done, q0 + tq - k_done))
        for si, (k0, width) in enumerate(spans):
            for e in range(2):
                items.append((q0, k0, width, k0 + width > q0, e, si == 0, si == len(spans) - 1))
    n_items = len(items)

    def logits_of(idx):
        q0, k0, width, _, e, _, _ = items[idx]
        return _dot(ka_scr[k0:k0 + width, :], qt_scr[e, :, q0:q0 + tq])

    m = {}

    def softmax_of(idx, logit_t):
        q0, k0, width, masked, e, first, _ = items[idx]
        if masked:
            lo = max(q0 - k0, 0)
            diag = logit_t[lo:] + tri_scr[k0 + lo - q0:k0 + width - q0, :]
            logit_t = diag if lo == 0 else jnp.concatenate([logit_t[:lo], diag], axis=0)
        m_blk = jnp.max(logit_t, axis=0, keepdims=True)
        m_new = m_blk if first else jnp.maximum(m[e], m_blk)
        alpha = None if first else jnp.exp2(m[e] - m_new)
        m[e] = m_new
        return alpha, jnp.exp2(logit_t - m_new).astype(BF16)

    acc = {}
    logits = {i: logits_of(i) for i in range(min(2, n_items))}
    probs = {0: softmax_of(0, logits.pop(0))}
    for idx, (q0, k0, width, masked, e, first, last) in enumerate(items):
        if idx + 2 < n_items:
            logits[idx + 2] = logits_of(idx + 2)
        if idx + 1 < n_items:
            probs[idx + 1] = softmax_of(idx + 1, logits.pop(idx + 1))
        alpha, p_t = probs.pop(idx)
        pv = _dot(vt_scr[e, :, k0:k0 + width], p_t)
        acc[e] = pv if first else alpha * acc[e] + pv
        if last and e == 1:
            o_t = jnp.concatenate([acc[h][0:hd] / acc[h][hd:hd + 1] for h in range(2)], axis=0)
            z_ref[q0:q0 + tq, :] = (_sigmoid(gb_ref[q0:q0 + tq, :].astype(F32)) * o_t.T).astype(BF16)


def _fox_call(proj, gates, batch, seq):
    n = batch * seq
    pairs = FOX_HEADS // 2
    per_group = D_MODEL // LANES

    def cols(group):
        return pl.BlockSpec((seq, LANES), lambda b, p: (b, group * per_group + p))

    return pl.pallas_call(
        functools.partial(_fox_kernel, seq=seq),
        grid=(batch, pairs),
        in_specs=[
            cols(COL_FX_Q),
            cols(COL_FX_K),
            cols(COL_FX_V),
            cols(COL_G_B),
            pl.BlockSpec((seq, GATE_COLS), lambda b, p: (b, 0)),
        ],
        out_specs=pl.BlockSpec((seq, LANES), lambda b, p: (b, p)),
        out_shape=jax.ShapeDtypeStruct((n, D_MODEL), BF16),
        scratch_shapes=[
            pltpu.VMEM((seq, LANES), BF16),
            pltpu.VMEM((seq, 2 * LANES), BF16),
            pltpu.VMEM((2, 2 * LANES, seq), BF16),
            pltpu.VMEM((2, FOX_VROWS, seq), BF16),
            pltpu.VMEM((FOX_TQ, FOX_TQ), F32),
        ],
        compiler_params=pltpu.CompilerParams(
            dimension_semantics=("parallel", "arbitrary"), vmem_limit_bytes=VMEM_LIMIT),
        name="fox",
    )(proj, proj, proj, proj, gates)


def _layer(x2d, batch, seq, ffn1_norm, ffn1_w_gate, ffn1_w_up, ffn1_w_down, mix_norm, w_in, b_in,
           conv_w, conv_b, ml_head_norm, w_out, ffn2_norm, ffn2_w_gate, ffn2_w_up, ffn2_w_down, final_g):
    d = D_MODEL
    x1, h1 = _ffn_call(x2d, ffn1_norm[None, :], ffn1_w_gate.astype(BF16), ffn1_w_up.astype(BF16),
                       ffn1_w_down.astype(BF16), next_g=mix_norm[None, :])

    sizes = (d, d, d, d, ML_HEADS, ML_HEADS, d, d, d, FOX_HEADS, d, d)
    starts = [0]
    for s in sizes[:-1]:
        starts.append(starts[-1] + s)
    (ml_q, ml_k, ml_v, ml_o, ml_i, ml_f, fx_q, fx_k, fx_v, fx_f, g_a, g_b) = [
        slice(a, a + s) for a, s in zip(starts, sizes)]
    order = [None] * N_COL_GROUPS
    for col, grp in ((COL_ML_Q, ml_q), (COL_ML_K, ml_k), (COL_ML_V, ml_v), (COL_ML_O, ml_o), (COL_G_A, g_a),
                     (COL_G_B, g_b), (COL_FX_Q, fx_q), (COL_FX_K, fx_k), (COL_FX_V, fx_v)):
        order[col] = grp
    w_t = w_in.T
    assert all(c.start % SUBLANES == 0 for c in order)
    w_main_t = _pack_w_call(w_t, jnp.asarray([c.start // SUBLANES for c in order], jnp.int32))
    b_main = jnp.concatenate([b_in[c] for c in order])[None, :]
    col_scale = jnp.ones((N_COL_GROUPS, d), F32).at[COL_FX_Q].set(LOG2E * FOX_HEAD_DIM ** -0.5).reshape(1, -1)
    pad = GATE_COLS - 2 * ML_HEADS - FOX_HEADS
    w_gates_t = jnp.concatenate([w_t[ml_i], w_t[ml_f], w_t[fx_f], jnp.zeros((pad, d), w_in.dtype)], axis=0)
    b_gates = jnp.concatenate([b_in[ml_i], b_in[ml_f], b_in[fx_f], jnp.zeros((pad,), b_in.dtype)])[None, :]
    proj, gates = _proj_call(h1, w_main_t, b_main, col_scale, w_gates_t, b_gates)

    z_a = _mlstm_call(proj, gates, conv_w, conv_b[None, :], ml_head_norm[None, :], batch, seq)
    z_b = _fox_call(proj, gates, batch, seq)

    return _ffn_call(x1, ffn2_norm[None, :], ffn2_w_gate.astype(BF16), ffn2_w_up.astype(BF16),
                     ffn2_w_down.astype(BF16), pre=(z_a, z_b, w_out.astype(BF16)), final_g=final_g)


def kernel(x, ffn1_norm, ffn1_w_gate, ffn1_w_up, ffn1_w_down, mix_norm, w_in, b_in, conv_w, conv_b,
           ml_head_norm, w_out, ffn2_norm, ffn2_w_gate, ffn2_w_up, ffn2_w_down, final_norm):
    batch, seq, d = x.shape
    assert d == D_MODEL and ffn1_norm.shape[0] == 1, "single-layer configuration"
    out = _layer(x.reshape(batch * seq, d), batch, seq, ffn1_norm[0], ffn1_w_gate[0], ffn1_w_up[0],
                 ffn1_w_down[0], mix_norm[0], w_in[0], b_in[0], conv_w[0], conv_b[0], ml_head_norm[0],
                 w_out[0], ffn2_norm[0], ffn2_w_gate[0], ffn2_w_up[0], ffn2_w_down[0], final_norm[None, :])
    return out.reshape(batch, seq, d)
```

```python
import functools
import math

import jax
import jax.numpy as jnp
from jax import lax
from jax.experimental import pallas as pl
from jax.experimental.pallas import tpu as pltpu

D_MODEL = 1024
ML_HEADS = 4
ML_HEAD_DIM = 256
CONV_WIDTH = 4
FOX_HEADS = 16
FOX_HEAD_DIM = 64
D_FF = 2816
EPS = 1e-6

LANES = 128
SUBLANES = 8
GATE_COLS = LANES
ML_I0, ML_F0, FX_F0 = 0, ML_HEADS, 2 * ML_HEADS

(COL_ML_Q, COL_ML_K, COL_ML_V, COL_ML_O, COL_G_A, COL_G_B, COL_FX_Q, COL_FX_K, COL_FX_V) = range(9)
N_COL_GROUPS = 9
LOG2E = 1.4426950408889634
K_SCALE_LOG = 0.5 * math.log(ML_HEAD_DIM)

FFN_TM = 1024
FFN_SUB = 512
FFN_TF = 256
PROJ_TM = 1024
PROJ_TN = 4608
ML_CHUNK = 256
HALO = 16
ML_PAD = 16
FOX_TQ = 256
FOX_TK = 2048
FOX_FEAT = 8
FOX_VROWS = 80
CS_BLK = 256
VMEM_LIMIT = 56 * 1024 * 1024

F32 = jnp.float32
BF16 = jnp.bfloat16


def _dot(a, b):
    return jnp.dot(a, b, preferred_element_type=F32)


def _dot_nt(a, b):
    return lax.dot_general(a, b, (((1,), (1,)), ((), ())), preferred_element_type=F32)


def _rmsnorm(x, g):
    return x * lax.rsqrt(jnp.mean(x * x, axis=-1, keepdims=True) + EPS) * g


def _sigmoid(x):
    return 1.0 / (1.0 + jnp.exp2(x * (-LOG2E)))


def _log_sigmoid(x):
    return jnp.minimum(x, 0.0) - jnp.log(1.0 + jnp.exp(-jnp.abs(x)))


def _cumsum_rows(x, tril):
    hi = x.astype(BF16)
    r1 = x - hi.astype(F32)
    mid = r1.astype(BF16)
    lo = (r1 - mid.astype(F32)).astype(BF16)
    return _dot(tril, hi) + _dot(tril, mid) + _dot(tril, lo)


def _tril(n):
    r = lax.broadcasted_iota(jnp.int32, (n, n), 0)
    c = lax.broadcasted_iota(jnp.int32, (n, n), 1)
    return r >= c


def _ffn_kernel(*refs, pre_proj, final_norm, next_norm):
    refs = list(refs)
    x_ref = refs.pop(0)
    if pre_proj:
        za_ref, zb_ref, wo_ref = refs.pop(0), refs.pop(0), refs.pop(0)
    g_ref, wg_ref, wu_ref, wd_ref = refs.pop(0), refs.pop(0), refs.pop(0), refs.pop(0)
    if final_norm:
        gf_ref = refs.pop(0)
    if next_norm:
        gn_ref = refs.pop(0)
    o_ref = refs.pop(0)
    if next_norm:
        hn_ref = refs.pop(0)
    (a_scr,) = refs

    def prologue(s):
        rows = slice(s * FFN_SUB, (s + 1) * FFN_SUB)
        x = x_ref[rows, :]
        if pre_proj:
            y = (za_ref[rows, :].astype(F32) + zb_ref[rows, :].astype(F32)).astype(BF16)
            x = x + _dot(y, wo_ref[...])
        return x, _rmsnorm(x, g_ref[...]).astype(BF16)

    def epilogue(s, x, y):
        rows = slice(s * FFN_SUB, (s + 1) * FFN_SUB)
        out = x + 0.5 * y
        if next_norm:
            hn_ref[rows, :] = _rmsnorm(out, gn_ref[...]).astype(BF16)
        if final_norm:
            out = _rmsnorm(out, gf_ref[...])
        o_ref[rows, :] = out

    n_sub = FFN_TM // FFN_SUB
    x, h = prologue(0)
    prev = None
    for s in range(n_sub):
        nxt = None
        for j in range(D_FF // FFN_TF):
            cols = slice(j * FFN_TF, (j + 1) * FFN_TF)
            gate = _dot(h, wg_ref[:, cols])
            up = _dot(h, wu_ref[:, cols])
            a_scr[s % 2, :, cols] = (gate * _sigmoid(gate) * up).astype(BF16)
            if j == 0 and prev is not None:
                prev = prev + (_dot(a_scr[(s - 1) % 2], wd_ref[...]),)
            if j == 1 and s + 1 < n_sub:
                nxt = prologue(s + 1)
            if j == 3 and prev is not None:
                epilogue(*prev)
                prev = None
        prev = (s, x)
        if nxt is not None:
            x, h = nxt
    epilogue(*prev, _dot(a_scr[(n_sub - 1) % 2], wd_ref[...]))


def _ffn_call(x, norm_g, wg, wu, wd, pre=None, final_g=None, next_g=None):
    n = x.shape[0]
    row = pl.BlockSpec((FFN_TM, D_MODEL), lambda i: (i, 0))
    vec = pl.BlockSpec((1, D_MODEL), lambda i: (0, 0))
    out_specs, out_shape = row, jax.ShapeDtypeStruct((n, D_MODEL), F32)
    if next_g is not None:
        out_specs, out_shape = [row, row], [out_shape, jax.ShapeDtypeStruct((n, D_MODEL), BF16)]

    def resident(shape):
        return pl.BlockSpec(shape, lambda i: (0, 0), pipeline_mode=pl.Buffered(1))

    args, specs = [x], [row]
    if pre is not None:
        za, zb, wo = pre
        args += [za, zb, wo]
        specs += [row, row, resident((D_MODEL, D_MODEL))]
    args += [norm_g, wg, wu, wd]
    specs += [vec, resident((D_MODEL, D_FF)), resident((D_MODEL, D_FF)), resident((D_FF, D_MODEL))]
    for g in (final_g, next_g):
        if g is not None:
            args.append(g)
            specs.append(vec)
    return pl.pallas_call(
        functools.partial(_ffn_kernel, pre_proj=pre is not None, final_norm=final_g is not None,
                          next_norm=next_g is not None),
        grid=(n // FFN_TM,),
        in_specs=specs,
        out_specs=out_specs,
        out_shape=out_shape,
        scratch_shapes=[pltpu.VMEM((2, FFN_SUB, D_FF), BF16)],
        compiler_params=pltpu.CompilerParams(
            dimension_semantics=("parallel",), vmem_limit_bytes=VMEM_LIMIT),
        name="ffn2" if pre is not None else "ffn1",
    )(*args)


def _pack_w_kernel(starts_ref, w_ref, o_ref):
    del starts_ref
    o_ref[...] = w_ref[...].astype(BF16)


def _pack_w_call(w_t, starts):
    n_groups = starts.shape[0]
    return pl.pallas_call(
        _pack_w_kernel,
        grid_spec=pltpu.PrefetchScalarGridSpec(
            num_scalar_prefetch=1,
            grid=(n_groups,),
            in_specs=[pl.BlockSpec((pl.Element(D_MODEL), pl.Element(D_MODEL)), lambda g, s: (s[g] * SUBLANES, 0))],
            out_specs=pl.BlockSpec((D_MODEL, D_MODEL), lambda g, s: (g, 0)),
        ),
        out_shape=jax.ShapeDtypeStruct((n_groups * D_MODEL, D_MODEL), BF16),
        compiler_params=pltpu.CompilerParams(dimension_semantics=("parallel",)),
        name="pack_w_in",
    )(starts, w_t)


def _proj_kernel(h_ref, w_ref, b_ref, s_ref, wgt_ref, bgt_ref, p_ref, gt_ref):
    h = h_ref[...]

    @pl.when(pl.program_id(1) == 0)
    def _():
        gt_ref[...] = _dot_nt(h, wgt_ref[...].astype(BF16)) + bgt_ref[...]

    p_ref[...] = ((_dot_nt(h, w_ref[...]) + b_ref[...]) * s_ref[...]).astype(BF16)


def _proj_call(h, w_t, b, col_scale, w_gates_t, b_gates):
    n = h.shape[0]
    n_out = w_t.shape[0]
    return pl.pallas_call(
        _proj_kernel,
        grid=(n // PROJ_TM, n_out // PROJ_TN),
        in_specs=[
            pl.BlockSpec((PROJ_TM, D_MODEL), lambda i, j: (i, 0)),
            pl.BlockSpec((PROJ_TN, D_MODEL), lambda i, j: (j, 0)),
            pl.BlockSpec((1, PROJ_TN), lambda i, j: (0, j)),
            pl.BlockSpec((1, PROJ_TN), lambda i, j: (0, j)),
            pl.BlockSpec((GATE_COLS, D_MODEL), lambda i, j: (0, 0)),
            pl.BlockSpec((1, GATE_COLS), lambda i, j: (0, 0)),
        ],
        out_specs=[
            pl.BlockSpec((PROJ_TM, PROJ_TN), lambda i, j: (i, j)),
            pl.BlockSpec((PROJ_TM, GATE_COLS), lambda i, j: (i, 0)),
        ],
        out_shape=[
            jax.ShapeDtypeStruct((n, n_out), BF16),
            jax.ShapeDtypeStruct((n, GATE_COLS), F32),
        ],
        compiler_params=pltpu.CompilerParams(
            dimension_semantics=("parallel", "arbitrary"), vmem_limit_bytes=VMEM_LIMIT),
        name="in_proj",
    )(h, w_t, b, col_scale, w_gates_t, b_gates)


def _mlstm_kernel(qk_ref, o_ref, ga_ref, v_ref, gt_ref, cw_ref, cb_ref, ln_ref, z_ref,
                  halo_scr, c_scr, m_scr):
    L = ML_CHUNK
    dh = ML_HEAD_DIM

    @pl.when(pl.program_id(1) == 0)
    def _():
        halo_scr[...] = jnp.zeros_like(halo_scr)
        c_scr[...] = jnp.zeros_like(c_scr)
        m_scr[...] = jnp.zeros_like(m_scr)

    u16 = qk_ref[...]
    halo16 = halo_scr[...]
    t_out = lax.broadcasted_iota(jnp.int32, (L, L), 0)
    t_in = lax.broadcasted_iota(jnp.int32, (L, L), 1)
    h_out = lax.broadcasted_iota(jnp.int32, (HALO, HALO), 0)
    h_in = lax.broadcasted_iota(jnp.int32, (HALO, HALO), 1)
    y = cb_ref[...] + cw_ref[CONV_WIDTH - 1:CONV_WIDTH, :] * u16.astype(F32)
    head_fix = jnp.zeros((HALO, 2 * D_MODEL), F32)
    for d in range(1, CONV_WIDTH):
        w_d = cw_ref[CONV_WIDTH - 1 - d:CONV_WIDTH - d, :]
        y = y + w_d * _dot(jnp.where(t_in == t_out - d, 1.0, 0.0).astype(BF16), u16)
        head_fix = head_fix + w_d * _dot(jnp.where(h_in == h_out - d + HALO, 1.0, 0.0).astype(BF16), halo16)
    y = jnp.concatenate([y[0:HALO] + head_fix, y[HALO:]], axis=0)
    halo_scr[...] = u16[L - HALO:L, :]
    qk = y * _sigmoid(y)

    gates = gt_ref[...]
    tril = _tril(L)
    bcum = _cumsum_rows(_log_sigmoid(gates), tril.astype(BF16))
    gates_t = gates.T
    bcum_t = bcum.T
    pad_row = lax.broadcasted_iota(jnp.int32, (ML_PAD, L), 0)
    ones_pad = jnp.where(pad_row == 0, 1.0, 0.0).astype(BF16)

    def matmul_stage(h):
        q_t = qk[:, h * dh:(h + 1) * dh].astype(BF16).T
        k16 = qk[:, D_MODEL + h * dh:D_MODEL + (h + 1) * dh].astype(BF16)
        state = c_scr[h]
        return q_t, k16, state, _dot(k16, q_t), _dot(state.astype(BF16), q_t)

    staged = matmul_stage(0)
    for h in range(ML_HEADS):
        cols = slice(h * dh, (h + 1) * dh)
        q_t, k16, state, qk_t, inter_t = staged
        if h + 1 < ML_HEADS:
            staged = matmul_stage(h + 1)
        v_aug = jnp.concatenate([v_ref[:, cols].T, ones_pad], axis=0)
        li_row = gates_t[ML_I0 + h:ML_I0 + h + 1, :]
        b_row = bcum_t[ML_F0 + h:ML_F0 + h + 1, :]
        ib_col = gates[:, ML_I0 + h:ML_I0 + h + 1] - bcum[:, ML_F0 + h:ML_F0 + h + 1]
        m_prev = m_scr[h][:, 0:1]

        dmat_t = jnp.where(t_out <= t_in, b_row + ib_col, -jnp.inf)
        inter = b_row + m_prev
        m_t = jnp.maximum(jnp.max(dmat_t, axis=0, keepdims=True), inter)
        w_inter = jnp.exp(inter - m_t)
        s_t = (qk_t * jnp.exp(dmat_t - (m_t + K_SCALE_LOG))).astype(BF16)
        num_den = _dot(v_aug, s_t) + w_inter * inter_t
        den = num_den[dh:dh + 1, :]
        hh_t = num_den[0:dh, :] / jnp.maximum(jnp.abs(den), jnp.exp(-m_t))

        b_last = b_row[:, L - 1:L]
        log_w = b_last - b_row + li_row
        m_new = jnp.maximum(b_last + m_prev, jnp.max(log_w, axis=1, keepdims=True))
        decay = jnp.exp(b_last + m_prev - m_new)
        w = jnp.exp(log_w - m_new) * (dh ** -0.5)
        c_scr[h] = decay * state + _dot((v_aug.astype(F32) * w).astype(BF16), k16)
        m_scr[h] = jnp.broadcast_to(m_new, (1, LANES))

        mu = jnp.mean(hh_t, axis=0, keepdims=True)
        dlt = hh_t - mu
        var = jnp.mean(dlt * dlt, axis=0, keepdims=True)
        hn = (dlt * lax.rsqrt(var + EPS)).T * ln_ref[:, cols]
        gate = _sigmoid(ga_ref[:, cols].astype(F32)) * _sigmoid(o_ref[:, cols].astype(F32))
        z_ref[:, cols] = (gate * hn).astype(BF16)


def _mlstm_call(proj, gates, conv_w, conv_b, ln_g, batch, seq):
    n = batch * seq
    nc = seq // ML_CHUNK
    L = ML_CHUNK

    def row(width, col):
        return pl.BlockSpec((L, width), lambda b, c: (b * nc + c, col))

    def full(shape):
        return pl.BlockSpec(shape, lambda b, c: (0, 0))

    return pl.pallas_call(
        _mlstm_kernel,
        grid=(batch, nc),
        in_specs=[
            row(2 * D_MODEL, COL_ML_Q // 2),
            row(D_MODEL, COL_ML_O),
            row(D_MODEL, COL_G_A),
            row(D_MODEL, COL_ML_V),
            row(GATE_COLS, 0),
            full((CONV_WIDTH, 2 * D_MODEL)),
            full((1, 2 * D_MODEL)),
            full((1, D_MODEL)),
        ],
        out_specs=row(D_MODEL, 0),
        out_shape=jax.ShapeDtypeStruct((n, D_MODEL), BF16),
        scratch_shapes=[
            pltpu.VMEM((HALO, 2 * D_MODEL), BF16),
            pltpu.VMEM((ML_HEADS, ML_HEAD_DIM + ML_PAD, ML_HEAD_DIM), F32),
            pltpu.VMEM((ML_HEADS, 1, LANES), F32),
        ],
        compiler_params=pltpu.CompilerParams(
            dimension_semantics=("parallel", "arbitrary"), vmem_limit_bytes=VMEM_LIMIT),
        name="mlstm",
    )(proj, proj, proj, proj, gates, conv_w, conv_b, ln_g)


def _fox_bias_features(gt_ref, qf_scr, ka_scr, seq):
    tril = _tril(CS_BLK).astype(BF16)
    r = lax.broadcasted_iota(jnp.int32, (GATE_COLS, 2 * LANES), 0)
    c = lax.broadcasted_iota(jnp.int32, (GATE_COLS, 2 * LANES), 1)
    head = r - FX_F0
    is_head = (head >= 0) & (head < FOX_HEADS)
    sel = []
    for piece in range(3):
        q_hit = is_head & (c == FOX_FEAT * head + piece)
        k_hit = is_head & (c == LANES + FOX_FEAT * head + 3 + piece)
        sel.append(jnp.where(q_hit, 1.0, jnp.where(k_hit, -1.0, 0.0)).astype(BF16))
    lane = lax.broadcasted_iota(jnp.int32, (1, 2 * LANES), 1)
    slot = lane % FOX_FEAT
    in_use = (lane % LANES) < FOX_FEAT * FOX_HEADS
    q_one = (lane < LANES) & (slot >= 3) & (slot < 6)
    k_one = (lane >= LANES) & (slot < 3)
    ones = jnp.where(in_use & (q_one | k_one), 1.0, 0.0)

    carry = jnp.zeros((1, GATE_COLS), F32)
    for blk in range(seq // CS_BLK):
        rows = slice(blk * CS_BLK, (blk + 1) * CS_BLK)
        cs = _cumsum_rows(_log_sigmoid(gt_ref[rows, :]), tril) + carry
        carry = cs[CS_BLK - 1:CS_BLK, :]
        cs2 = cs * LOG2E
        hi = cs2.astype(BF16)
        r1 = cs2 - hi.astype(F32)
        mid = r1.astype(BF16)
        lo = (r1 - mid.astype(F32)).astype(BF16)
        feat = (_dot(hi, sel[0]) + _dot(mid, sel[1]) + _dot(lo, sel[2]) + ones).astype(BF16)
        qf_scr[rows, :] = feat[:, 0:LANES]
        ka_scr[rows, LANES:2 * LANES] = feat[:, LANES:2 * LANES]


def _fox_kernel(q_ref, k_ref, v_ref, gb_ref, gt_ref, z_ref, qf_scr, ka_scr, qt_scr, vt_scr, tri_scr, *, seq):
    tq = FOX_TQ
    hd = FOX_HEAD_DIM
    pair = pl.program_id(1)

    @pl.when(pair == 0)
    def _():
        _fox_bias_features(gt_ref, qf_scr, ka_scr, seq)
        key = lax.broadcasted_iota(jnp.int32, (tq, tq), 0)
        qry = lax.broadcasted_iota(jnp.int32, (tq, tq), 1)
        tri_scr[...] = jnp.where(qry >= key, 0.0, -jnp.inf)
        one_row = lax.broadcasted_iota(jnp.int32, (FOX_VROWS - hd, seq), 0) == 0
        for e in range(2):
            vt_scr[e, hd:FOX_VROWS, :] = jnp.where(one_row, 1.0, 0.0).astype(BF16)

    ka_scr[:, 0:LANES] = k_ref[...]
    sub = lax.broadcasted_iota(jnp.int32, (2 * LANES, 1), 0)
    keep = []
    for e in range(2):
        feat0 = LANES + FOX_FEAT * (2 * pair + e)
        own_q = (sub >= e * FOX_HEAD_DIM) & (sub < (e + 1) * FOX_HEAD_DIM)
        own_feat = (sub >= feat0) & (sub < feat0 + FOX_FEAT)
        keep.append(jnp.where(own_q | own_feat, 1.0, 0.0).astype(BF16))
    for blk in range(seq // tq):
        rows = slice(blk * tq, (blk + 1) * tq)
        q_aug_t = jnp.concatenate([q_ref[rows, :], qf_scr[rows, :]], axis=1).T
        v_t = v_ref[rows, :].T
        for e in range(2):
            qt_scr[e, :, rows] = q_aug_t * keep[e]
            vt_scr[e, 0:hd, rows] = v_t[e * hd:(e + 1) * hd, :]

    items = []
    for qi in range(seq // tq):
        q0 = qi * tq
        if FOX_TK <= tq:
            spans = [(k0, FOX_TK) for k0 in range(0, q0 + tq, FOX_TK)]
        else:
            spans = [(k0, FOX_TK) for k0 in range(0, q0 - FOX_TK + 1, FOX_TK)]
            k_done = len(spans) * FOX_TK
            spans.append((k_done, q0 + tq - k_done))
        for si, (k0, width) in enumerate(spans):
            for e in range(2):
                items.append((q0, k0, width, k0 + width > q0, e, si == 0, si == len(spans) - 1))
    n_items = len(items)

    def logits_of(idx):
        q0, k0, width, _, e, _, _ = items[idx]
        return _dot(ka_scr[k0:k0 + width, :], qt_scr[e, :, q0:q0 + tq])

    m = {}

    def softmax_of(idx, logit_t):
        q0, k0, width, masked, e, first, _ = items[idx]
        if masked:
            lo = max(q0 - k0, 0)
            diag = logit_t[lo:] + tri_scr[k0 + lo - q0:k0 + width - q0, :]
            logit_t = diag if lo == 0 else jnp.concatenate([logit_t[:lo], diag], axis=0)
        m_blk = jnp.max(logit_t, axis=0, keepdims=True)
        m_new = m_blk if first else jnp.maximum(m[e], m_blk)
        alpha = None if first else jnp.exp2(m[e] - m_new)
        m[e] = m_new
        return alpha, jnp.exp2(logit_t - m_new).astype(BF16)

    acc = {}
    logits = {i: logits_of(i) for i in range(min(2, n_items))}
    probs = {0: softmax_of(0, logits.pop(0))}
    for idx, (q0, k0, width, masked, e, first, last) in enumerate(items):
        if idx + 2 < n_items:
            logits[idx + 2] = logits_of(idx + 2)
        if idx + 1 < n_items:
            probs[idx + 1] = softmax_of(idx + 1, logits.pop(idx + 1))
        alpha, p_t = probs.pop(idx)
        pv = _dot(vt_scr[e, :, k0:k0 + width], p_t)
        acc[e] = pv if first else alpha * acc[e] + pv
        if last and e == 1:
            o_t = jnp.concatenate([acc[h][0:hd] / acc[h][hd:hd + 1] for h in range(2)], axis=0)
            z_ref[q0:q0 + tq, :] = (_sigmoid(gb_ref[q0:q0 + tq, :].astype(F32)) * o_t.T).astype(BF16)


def _fox_call(proj, gates, batch, seq):
    n = batch * seq
    pairs = FOX_HEADS // 2
    per_group = D_MODEL // LANES

    def cols(group):
        return pl.BlockSpec((seq, LANES), lambda b, p: (b, group * per_group + p))

    return pl.pallas_call(
        functools.partial(_fox_kernel, seq=seq),
        grid=(batch, pairs),
        in_specs=[
            cols(COL_FX_Q),
            cols(COL_FX_K),
            cols(COL_FX_V),
            cols(COL_G_B),
            pl.BlockSpec((seq, GATE_COLS), lambda b, p: (b, 0)),
        ],
        out_specs=pl.BlockSpec((seq, LANES), lambda b, p: (b, p)),
        out_shape=jax.ShapeDtypeStruct((n, D_MODEL), BF16),
        scratch_shapes=[
            pltpu.VMEM((seq, LANES), BF16),
            pltpu.VMEM((seq, 2 * LANES), BF16),
            pltpu.VMEM((2, 2 * LANES, seq), BF16),
            pltpu.VMEM((2, FOX_VROWS, seq), BF16),
            pltpu.VMEM((FOX_TQ, FOX_TQ), F32),
        ],
        compiler_params=pltpu.CompilerParams(
            dimension_semantics=("parallel", "arbitrary"), vmem_limit_bytes=VMEM_LIMIT),
        name="fox",
    )(proj, proj, proj, proj, gates)


def _layer(x2d, batch, seq, ffn1_norm, ffn1_w_gate, ffn1_w_up, ffn1_w_down, mix_norm, w_in, b_in,
           conv_w, conv_b, ml_head_norm, w_out, ffn2_norm, ffn2_w_gate, ffn2_w_up, ffn2_w_down, final_g):
    d = D_MODEL
    x1, h1 = _ffn_call(x2d, ffn1_norm[None, :], ffn1_w_gate.astype(BF16), ffn1_w_up.astype(BF16),
                       ffn1_w_down.astype(BF16), next_g=mix_norm[None, :])

    sizes = (d, d, d, d, ML_HEADS, ML_HEADS, d, d, d, FOX_HEADS, d, d)
    starts = [0]
    for s in sizes[:-1]:
        starts.append(starts[-1] + s)
    (ml_q, ml_k, ml_v, ml_o, ml_i, ml_f, fx_q, fx_k, fx_v, fx_f, g_a, g_b) = [
        slice(a, a + s) for a, s in zip(starts, sizes)]
    order = [None] * N_COL_GROUPS
    for col, grp in ((COL_ML_Q, ml_q), (COL_ML_K, ml_k), (COL_ML_V, ml_v), (COL_ML_O, ml_o), (COL_G_A, g_a),
                     (COL_G_B, g_b), (COL_FX_Q, fx_q), (COL_FX_K, fx_k), (COL_FX_V, fx_v)):
        order[col] = grp
    w_t = w_in.T
    assert all(c.start % SUBLANES == 0 for c in order)
    w_main_t = _pack_w_call(w_t, jnp.asarray([c.start // SUBLANES for c in order], jnp.int32))
    b_main = jnp.concatenate([b_in[c] for c in order])[None, :]
    col_scale = jnp.ones((N_COL_GROUPS, d), F32).at[COL_FX_Q].set(LOG2E * FOX_HEAD_DIM ** -0.5).reshape(1, -1)
    pad = GATE_COLS - 2 * ML_HEADS - FOX_HEADS
    w_gates_t = jnp.concatenate([w_t[ml_i], w_t[ml_f], w_t[fx_f], jnp.zeros((pad, d), w_in.dtype)], axis=0)
    b_gates = jnp.concatenate([b_in[ml_i], b_in[ml_f], b_in[fx_f], jnp.zeros((pad,), b_in.dtype)])[None, :]
    proj, gates = _proj_call(h1, w_main_t, b_main, col_scale, w_gates_t, b_gates)

    z_a = _mlstm_call(proj, gates, conv_w, conv_b[None, :], ml_head_norm[None, :], batch, seq)
    z_b = _fox_call(proj, gates, batch, seq)

    return _ffn_call(x1, ffn2_norm[None, :], ffn2_w_gate.astype(BF16), ffn2_w_up.astype(BF16),
                     ffn2_w_down.astype(BF16), pre=(z_a, z_b, w_out.astype(BF16)), final_g=final_g)


def kernel(x, ffn1_norm, ffn1_w_gate, ffn1_w_up, ffn1_w_down, mix_norm, w_in, b_in, conv_w, conv_b,
           ml_head_norm, w_out, ffn2_norm, ffn2_w_gate, ffn2_w_up, ffn2_w_down, final_norm):
    batch, seq, d = x.shape
    assert d == D_MODEL and ffn1_norm.shape[0] == 1, "single-layer configuration"
    out = _layer(x.reshape(batch * seq, d), batch, seq, ffn1_norm[0], ffn1_w_gate[0], ffn1_w_up[0],
                 ffn1_w_down[0], mix_norm[0], w_in[0], b_in[0], conv_w[0], conv_b[0], ml_head_norm[0],
                 w_out[0], ffn2_norm[0], ffn2_w_gate[0], ffn2_w_up[0], ffn2_w_down[0], final_norm[None, :])
    return out.reshape(batch, seq, d)
```

```python
import functools
import math

import jax
import jax.numpy as jnp
from jax import lax
from jax.experimental import pallas as pl
from jax.experimental.pallas import tpu as pltpu

D_MODEL = 1024
ML_HEADS = 4
ML_HEAD_DIM = 256
CONV_WIDTH = 4
FOX_HEADS = 16
FOX_HEAD_DIM = 64
D_FF = 2816
EPS = 1e-6

LANES = 128
SUBLANES = 8
GATE_COLS = LANES
ML_I0, ML_F0, FX_F0 = 0, ML_HEADS, 2 * ML_HEADS

(COL_ML_Q, COL_ML_K, COL_ML_V, COL_ML_O, COL_G_A, COL_G_B, COL_FX_Q, COL_FX_K, COL_FX_V) = range(9)
N_COL_GROUPS = 9
LOG2E = 1.4426950408889634
K_SCALE_LOG = 0.5 * math.log(ML_HEAD_DIM)

FFN_TM = 1024
FFN_SUB = 512
FFN_TF = 256
PROJ_TM = 1024
PROJ_TN = 4608
ML_CHUNK = 256
HALO = 16
ML_PAD = 16
FOX_TQ = 256
FOX_TK = 2048
FOX_FEAT = 8
FOX_VROWS = 80
CS_BLK = 256
VMEM_LIMIT = 56 * 1024 * 1024

F32 = jnp.float32
BF16 = jnp.bfloat16


def _dot(a, b):
    return jnp.dot(a, b, preferred_element_type=F32)


def _dot_nt(a, b):
    return lax.dot_general(a, b, (((1,), (1,)), ((), ())), preferred_element_type=F32)


def _rmsnorm(x, g):
    return x * lax.rsqrt(jnp.mean(x * x, axis=-1, keepdims=True) + EPS) * g


def _sigmoid(x):
    return 1.0 / (1.0 + jnp.exp2(x * (-LOG2E)))


def _log_sigmoid(x):
    return jnp.minimum(x, 0.0) - jnp.log(1.0 + jnp.exp(-jnp.abs(x)))


def _cumsum_rows(x, tril):
    hi = x.astype(BF16)
    r1 = x - hi.astype(F32)
    mid = r1.astype(BF16)
    lo = (r1 - mid.astype(F32)).astype(BF16)
    return _dot(tril, hi) + _dot(tril, mid) + _dot(tril, lo)


def _tril(n):
    r = lax.broadcasted_iota(jnp.int32, (n, n), 0)
    c = lax.broadcasted_iota(jnp.int32, (n, n), 1)
    return r >= c


def _ffn_kernel(*refs, pre_proj, final_norm, next_norm):
    refs = list(refs)
    x_ref = refs.pop(0)
    if pre_proj:
        za_ref, zb_ref, wo_ref = refs.pop(0), refs.pop(0), refs.pop(0)
    g_ref, wg_ref, wu_ref, wd_ref = refs.pop(0), refs.pop(0), refs.pop(0), refs.pop(0)
    if final_norm:
        gf_ref = refs.pop(0)
    if next_norm:
        gn_ref = refs.pop(0)
    o_ref = refs.pop(0)
    if next_norm:
        hn_ref = refs.pop(0)
    (a_scr,) = refs

    def prologue(s):
        rows = slice(s * FFN_SUB, (s + 1) * FFN_SUB)
        x = x_ref[rows, :]
        if pre_proj:
            y = (za_ref[rows, :].astype(F32) + zb_ref[rows, :].astype(F32)).astype(BF16)
            x = x + _dot(y, wo_ref[...])
        return x, _rmsnorm(x, g_ref[...]).astype(BF16)

    def epilogue(s, x, y):
        rows = slice(s * FFN_SUB, (s + 1) * FFN_SUB)
        out = x + 0.5 * y
        if next_norm:
            hn_ref[rows, :] = _rmsnorm(out, gn_ref[...]).astype(BF16)
        if final_norm:
            out = _rmsnorm(out, gf_ref[...])
        o_ref[rows, :] = out

    n_sub = FFN_TM // FFN_SUB
    x, h = prologue(0)
    prev = None
    for s in range(n_sub):
        nxt = None
        for j in range(D_FF // FFN_TF):
            cols = slice(j * FFN_TF, (j + 1) * FFN_TF)
            gate = _dot(h, wg_ref[:, cols])
            up = _dot(h, wu_ref[:, cols])
            a_scr[s % 2, :, cols] = (gate * _sigmoid(gate) * up).astype(BF16)
            if j == 0 and prev is not None:
                prev = prev + (_dot(a_scr[(s - 1) % 2], wd_ref[...]),)
            if j == 1 and s + 1 < n_sub:
                nxt = prologue(s + 1)
            if j == 3 and prev is not None:
                epilogue(*prev)
                prev = None
        prev = (s, x)
        if nxt is not None:
            x, h = nxt
    epilogue(*prev, _dot(a_scr[(n_sub - 1) % 2], wd_ref[...]))


def _ffn_call(x, norm_g, wg, wu, wd, pre=None, final_g=None, next_g=None):
    n = x.shape[0]
    row = pl.BlockSpec((FFN_TM, D_MODEL), lambda i: (i, 0))
    vec = pl.BlockSpec((1, D_MODEL), lambda i: (0, 0))
    out_specs, out_shape = row, jax.ShapeDtypeStruct((n, D_MODEL), F32)
    if next_g is not None:
        out_specs, out_shape = [row, row], [out_shape, jax.ShapeDtypeStruct((n, D_MODEL), BF16)]

    def resident(shape):
        return pl.BlockSpec(shape, lambda i: (0, 0), pipeline_mode=pl.Buffered(1))

    args, specs = [x], [row]
    if pre is not None:
        za, zb, wo = pre
        args += [za, zb, wo]
        specs += [row, row, resident((D_MODEL, D_MODEL))]
    args += [norm_g, wg, wu, wd]
    specs += [vec, resident((D_MODEL, D_FF)), resident((D_MODEL, D_FF)), resident((D_FF, D_MODEL))]
    for g in (final_g, next_g):
        if g is not None:
            args.append(g)
            specs.append(vec)
    return pl.pallas_call(
        functools.partial(_ffn_kernel, pre_proj=pre is not None, final_norm=final_g is not None,
                          next_norm=next_g is not None),
        grid=(n // FFN_TM,),
        in_specs=specs,
        out_specs=out_specs,
        out_shape=out_shape,
        scratch_shapes=[pltpu.VMEM((2, FFN_SUB, D_FF), BF16)],
        compiler_params=pltpu.CompilerParams(
            dimension_semantics=("parallel",), vmem_limit_bytes=VMEM_LIMIT),
        name="ffn2" if pre is not None else "ffn1",
    )(*args)


def _pack_w_kernel(starts_ref, w_ref, o_ref):
    del starts_ref
    o_ref[...] = w_ref[...].astype(BF16)


def _pack_w_call(w_t, starts):
    n_groups = starts.shape[0]
    return pl.pallas_call(
        _pack_w_kernel,
        grid_spec=pltpu.PrefetchScalarGridSpec(
            num_scalar_prefetch=1,
            grid=(n_groups,),
            in_specs=[pl.BlockSpec((pl.Element(D_MODEL), pl.Element(D_MODEL)), lambda g, s: (s[g] * SUBLANES, 0))],
            out_specs=pl.BlockSpec((D_MODEL, D_MODEL), lambda g, s: (g, 0)),
        ),
        out_shape=jax.ShapeDtypeStruct((n_groups * D_MODEL, D_MODEL), BF16),
        compiler_params=pltpu.CompilerParams(dimension_semantics=("parallel",)),
        name="pack_w_in",
    )(starts, w_t)


def _proj_kernel(h_ref, w_ref, b_ref, s_ref, wgt_ref, bgt_ref, p_ref, gt_ref):
    h = h_ref[...]

    @pl.when(pl.program_id(1) == 0)
    def _():
        gt_ref[...] = _dot_nt(h, wgt_ref[...].astype(BF16)) + bgt_ref[...]

    p_ref[...] = ((_dot_nt(h, w_ref[...]) + b_ref[...]) * s_ref[...]).astype(BF16)


def _proj_call(h, w_t, b, col_scale, w_gates_t, b_gates):
    n = h.shape[0]
    n_out = w_t.shape[0]
    return pl.pallas_call(
        _proj_kernel,
        grid=(n // PROJ_TM, n_out // PROJ_TN),
        in_specs=[
            pl.BlockSpec((PROJ_TM, D_MODEL), lambda i, j: (i, 0)),
            pl.BlockSpec((PROJ_TN, D_MODEL), lambda i, j: (j, 0)),
            pl.BlockSpec((1, PROJ_TN), lambda i, j: (0, j)),
            pl.BlockSpec((1, PROJ_TN), lambda i, j: (0, j)),
            pl.BlockSpec((GATE_COLS, D_MODEL), lambda i, j: (0, 0)),
            pl.BlockSpec((1, GATE_COLS), lambda i, j: (0, 0)),
        ],
        out_specs=[
            pl.BlockSpec((PROJ_TM, PROJ_TN), lambda i, j: (i, j)),
            pl.BlockSpec((PROJ_TM, GATE_COLS), lambda i, j: (i, 0)),
        ],
        out_shape=[
            jax.ShapeDtypeStruct((n, n_out), BF16),
            jax.ShapeDtypeStruct((n, GATE_COLS), F32),
        ],
        compiler_params=pltpu.CompilerParams(
            dimension_semantics=("parallel", "arbitrary"), vmem_limit_bytes=VMEM_LIMIT),
        name="in_proj",
    )(h, w_t, b, col_scale, w_gates_t, b_gates)


def _mlstm_kernel(qk_ref, o_ref, ga_ref, v_ref, gt_ref, cw_ref, cb_ref, ln_ref, z_ref,
                  halo_scr, c_scr, m_scr):
    L = ML_CHUNK
    dh = ML_HEAD_DIM

    @pl.when(pl.program_id(1) == 0)
    def _():
        halo_scr[...] = jnp.zeros_like(halo_scr)
        c_scr[...] = jnp.zeros_like(c_scr)
        m_scr[...] = jnp.zeros_like(m_scr)

    u16 = qk_ref[...]
    halo16 = halo_scr[...]
    t_out = lax.broadcasted_iota(jnp.int32, (L, L), 0)
    t_in = lax.broadcasted_iota(jnp.int32, (L, L), 1)
    h_out = lax.broadcasted_iota(jnp.int32, (HALO, HALO), 0)
    h_in = lax.broadcasted_iota(jnp.int32, (HALO, HALO), 1)
    y = cb_ref[...] + cw_ref[CONV_WIDTH - 1:CONV_WIDTH, :] * u16.astype(F32)
    head_fix = jnp.zeros((HALO, 2 * D_MODEL), F32)
    for d in range(1, CONV_WIDTH):
        w_d = cw_ref[CONV_WIDTH - 1 - d:CONV_WIDTH - d, :]
        y = y + w_d * _dot(jnp.where(t_in == t_out - d, 1.0, 0.0).astype(BF16), u16)
        head_fix = head_fix + w_d * _dot(jnp.where(h_in == h_out - d + HALO, 1.0, 0.0).astype(BF16), halo16)
    y = jnp.concatenate([y[0:HALO] + head_fix, y[HALO:]], axis=0)
    halo_scr[...] = u16[L - HALO:L, :]
    qk = y * _sigmoid(y)

    gates = gt_ref[...]
    tril = _tril(L)
    bcum = _cumsum_rows(_log_sigmoid(gates), tril.astype(BF16))
    gates_t = gates.T
    bcum_t = bcum.T
    pad_row = lax.broadcasted_iota(jnp.int32, (ML_PAD, L), 0)
    ones_pad = jnp.where(pad_row == 0, 1.0, 0.0).astype(BF16)

    def matmul_stage(h):
        q_t = qk[:, h * dh:(h + 1) * dh].astype(BF16).T
        k16 = qk[:, D_MODEL + h * dh:D_MODEL + (h + 1) * dh].astype(BF16)
        state = c_scr[h]
        return q_t, k16, state, _dot(k16, q_t), _dot(state.astype(BF16), q_t)

    staged = matmul_stage(0)
    for h in range(ML_HEADS):
        cols = slice(h * dh, (h + 1) * dh)
        q_t, k16, state, qk_t, inter_t = staged
        if h + 1 < ML_HEADS:
            staged = matmul_stage(h + 1)
        v_aug = jnp.concatenate([v_ref[:, cols].T, ones_pad], axis=0)
        li_row = gates_t[ML_I0 + h:ML_I0 + h + 1, :]
        b_row = bcum_t[ML_F0 + h:ML_F0 + h + 1, :]
        ib_col = gates[:, ML_I0 + h:ML_I0 + h + 1] - bcum[:, ML_F0 + h:ML_F0 + h + 1]
        m_prev = m_scr[h][:, 0:1]

        dmat_t = jnp.where(t_out <= t_in, b_row + ib_col, -jnp.inf)
        inter = b_row + m_prev
        m_t = jnp.maximum(jnp.max(dmat_t, axis=0, keepdims=True), inter)
        w_inter = jnp.exp(inter - m_t)
        s_t = (qk_t * jnp.exp(dmat_t - (m_t + K_SCALE_LOG))).astype(BF16)
        num_den = _dot(v_aug, s_t) + w_inter * inter_t
        den = num_den[dh:dh + 1, :]
        hh_t = num_den[0:dh, :] / jnp.maximum(jnp.abs(den), jnp.exp(-m_t))

        b_last = b_row[:, L - 1:L]
        log_w = b_last - b_row + li_row
        m_new = jnp.maximum(b_last + m_prev, jnp.max(log_w, axis=1, keepdims=True))
        decay = jnp.exp(b_last + m_prev - m_new)
        w = jnp.exp(log_w - m_new) * (dh ** -0.5)
        c_scr[h] = decay * state + _dot(v_aug * w.astype(BF16), k16)
        m_scr[h] = jnp.broadcast_to(m_new, (1, LANES))

        mu = jnp.mean(hh_t, axis=0, keepdims=True)
        dlt = hh_t - mu
        var = jnp.mean(dlt * dlt, axis=0, keepdims=True)
        hn = (dlt * lax.rsqrt(var + EPS)).T * ln_ref[:, cols]
        gate = _sigmoid(ga_ref[:, cols].astype(F32)) * _sigmoid(o_ref[:, cols].astype(F32))
        z_ref[:, cols] = (gate * hn).astype(BF16)


def _mlstm_call(proj, gates, conv_w, conv_b, ln_g, batch, seq):
    n = batch * seq
    nc = seq // ML_CHUNK
    L = ML_CHUNK

    def row(width, col):
        return pl.BlockSpec((L, width), lambda b, c: (b * nc + c, col))

    def full(shape):
        return pl.BlockSpec(shape, lambda b, c: (0, 0))

    return pl.pallas_call(
        _mlstm_kernel,
        grid=(batch, nc),
        in_specs=[
            row(2 * D_MODEL, COL_ML_Q // 2),
            row(D_MODEL, COL_ML_O),
            row(D_MODEL, COL_G_A),
            row(D_MODEL, COL_ML_V),
            row(GATE_COLS, 0),
            full((CONV_WIDTH, 2 * D_MODEL)),
            full((1, 2 * D_MODEL)),
            full((1, D_MODEL)),
        ],
        out_specs=row(D_MODEL, 0),
        out_shape=jax.ShapeDtypeStruct((n, D_MODEL), BF16),
        scratch_shapes=[
            pltpu.VMEM((HALO, 2 * D_MODEL), BF16),
            pltpu.VMEM((ML_HEADS, ML_HEAD_DIM + ML_PAD, ML_HEAD_DIM), F32),
            pltpu.VMEM((ML_HEADS, 1, LANES), F32),
        ],
        compiler_params=pltpu.CompilerParams(
            dimension_semantics=("parallel", "arbitrary"), vmem_limit_bytes=VMEM_LIMIT),
        name="mlstm",
    )(proj, proj, proj, proj, gates, conv_w, conv_b, ln_g)


def _fox_bias_features(gt_ref, qf_scr, ka_scr, seq):
    tril = _tril(CS_BLK).astype(BF16)
    r = lax.broadcasted_iota(jnp.int32, (GATE_COLS, 2 * LANES), 0)
    c = lax.broadcasted_iota(jnp.int32, (GATE_COLS, 2 * LANES), 1)
    head = r - FX_F0
    is_head = (head >= 0) & (head < FOX_HEADS)
    sel = []
    for piece in range(3):
        q_hit = is_head & (c == FOX_FEAT * head + piece)
        k_hit = is_head & (c == LANES + FOX_FEAT * head + 3 + piece)
        sel.append(jnp.where(q_hit, 1.0, jnp.where(k_hit, -1.0, 0.0)).astype(BF16))
    lane = lax.broadcasted_iota(jnp.int32, (1, 2 * LANES), 1)
    slot = lane % FOX_FEAT
    in_use = (lane % LANES) < FOX_FEAT * FOX_HEADS
    q_one = (lane < LANES) & (slot >= 3) & (slot < 6)
    k_one = (lane >= LANES) & (slot < 3)
    ones = jnp.where(in_use & (q_one | k_one), 1.0, 0.0)

    carry = jnp.zeros((1, GATE_COLS), F32)
    for blk in range(seq // CS_BLK):
        rows = slice(blk * CS_BLK, (blk + 1) * CS_BLK)
        cs = _cumsum_rows(_log_sigmoid(gt_ref[rows, :]), tril) + carry
        carry = cs[CS_BLK - 1:CS_BLK, :]
        cs2 = cs * LOG2E
        hi = cs2.astype(BF16)
        r1 = cs2 - hi.astype(F32)
        mid = r1.astype(BF16)
        lo = (r1 - mid.astype(F32)).astype(BF16)
        feat = (_dot(hi, sel[0]) + _dot(mid, sel[1]) + _dot(lo, sel[2]) + ones).astype(BF16)
        qf_scr[rows, :] = feat[:, 0:LANES]
        ka_scr[rows, LANES:2 * LANES] = feat[:, LANES:2 * LANES]


def _fox_kernel(q_ref, k_ref, v_ref, gb_ref, gt_ref, z_ref, qf_scr, ka_scr, qt_scr, vt_scr, tri_scr, *, seq):
    tq = FOX_TQ
    hd = FOX_HEAD_DIM
    pair = pl.program_id(1)

    @pl.when(pair == 0)
    def _():
        _fox_bias_features(gt_ref, qf_scr, ka_scr, seq)
        key = lax.broadcasted_iota(jnp.int32, (tq, tq), 0)
        qry = lax.broadcasted_iota(jnp.int32, (tq, tq), 1)
        tri_scr[...] = jnp.where(qry >= key, 0.0, -jnp.inf)
        one_row = lax.broadcasted_iota(jnp.int32, (FOX_VROWS - hd, seq), 0) == 0
        for e in range(2):
            vt_scr[e, hd:FOX_VROWS, :] = jnp.where(one_row, 1.0, 0.0).astype(BF16)

    ka_scr[:, 0:LANES] = k_ref[...]
    sub = lax.broadcasted_iota(jnp.int32, (2 * LANES, 1), 0)
    keep = []
    for e in range(2):
        feat0 = LANES + FOX_FEAT * (2 * pair + e)
        own_q = (sub >= e * FOX_HEAD_DIM) & (sub < (e + 1) * FOX_HEAD_DIM)
        own_feat = (sub >= feat0) & (sub < feat0 + FOX_FEAT)
        keep.append(jnp.where(own_q | own_feat, 1.0, 0.0).astype(BF16))
    for blk in range(seq // tq):
        rows = slice(blk * tq, (blk + 1) * tq)
        q_aug_t = jnp.concatenate([q_ref[rows, :], qf_scr[rows, :]], axis=1).T
        v_t = v_ref[rows, :].T
        for e in range(2):
            qt_scr[e, :, rows] = q_aug_t * keep[e]
            vt_scr[e, 0:hd, rows] = v_t[e * hd:(e + 1) * hd, :]

    items = []
    for qi in range(seq // tq):
        q0 = qi * tq
        if FOX_TK <= tq:
            spans = [(k0, FOX_TK) for k0 in range(0, q0 + tq, FOX_TK)]
        else:
            spans = [(k0, FOX_TK) for k0 in range(0, q0 - FOX_TK + 1, FOX_TK)]
            k_done = len(spans) * FOX_TK
            spans.append((k_done, q0 + tq - k_done))
        for si, (k0, width) in enumerate(spans):
            for e in range(2):
                items.append((q0, k0, width, k0 + width > q0, e, si == 0, si == len(spans) - 1))
    n_items = len(items)

    def logits_of(idx):
        q0, k0, width, _, e, _, _ = items[idx]
        return _dot(ka_scr[k0:k0 + width, :], qt_scr[e, :, q0:q0 + tq])

    m = {}

    def softmax_of(idx, logit_t):
        q0, k0, width, masked, e, first, _ = items[idx]
        if masked:
            lo = max(q0 - k0, 0)
            diag = logit_t[lo:] + tri_scr[k0 + lo - q0:k0 + width - q0, :]
            logit_t = diag if lo == 0 else jnp.concatenate([logit_t[:lo], diag], axis=0)
        m_blk = jnp.max(logit_t, axis=0, keepdims=True)
        m_new = m_blk if first else jnp.maximum(m[e], m_blk)
        alpha = None if first else jnp.exp2(m[e] - m_new)
        m[e] = m_new
        return alpha, jnp.exp2((logit_t - m_new).astype(BF16))

    acc = {}
    logits = {i: logits_of(i) for i in range(min(2, n_items))}
    probs = {0: softmax_of(0, logits.pop(0))}
    for idx, (q0, k0, width, masked, e, first, last) in enumerate(items):
        if idx + 2 < n_items:
            logits[idx + 2] = logits_of(idx + 2)
        if idx + 1 < n_items:
            probs[idx + 1] = softmax_of(idx + 1, logits.pop(idx + 1))
        alpha, p_t = probs.pop(idx)
        pv = _dot(vt_scr[e, :, k0:k0 + width], p_t)
        acc[e] = pv if first else alpha * acc[e] + pv
        if last and e == 1:
            o_t = jnp.concatenate([acc[h][0:hd] / acc[h][hd:hd + 1] for h in range(2)], axis=0)
            z_ref[q0:q0 + tq, :] = (_sigmoid(gb_ref[q0:q0 + tq, :].astype(F32)) * o_t.T).astype(BF16)


def _fox_call(proj, gates, batch, seq):
    n = batch * seq
    pairs = FOX_HEADS // 2
    per_group = D_MODEL // LANES

    def cols(group):
        return pl.BlockSpec((seq, LANES), lambda b, p: (b, group * per_group + p))

    return pl.pallas_call(
        functools.partial(_fox_kernel, seq=seq),
        grid=(batch, pairs),
        in_specs=[
            cols(COL_FX_Q),
            cols(COL_FX_K),
            cols(COL_FX_V),
            cols(COL_G_B),
            pl.BlockSpec((seq, GATE_COLS), lambda b, p: (b, 0)),
        ],
        out_specs=pl.BlockSpec((seq, LANES), lambda b, p: (b, p)),
        out_shape=jax.ShapeDtypeStruct((n, D_MODEL), BF16),
        scratch_shapes=[
            pltpu.VMEM((seq, LANES), BF16),
            pltpu.VMEM((seq, 2 * LANES), BF16),
            pltpu.VMEM((2, 2 * LANES, seq), BF16),
            pltpu.VMEM((2, FOX_VROWS, seq), BF16),
            pltpu.VMEM((FOX_TQ, FOX_TQ), F32),
        ],
        compiler_params=pltpu.CompilerParams(
            dimension_semantics=("parallel", "arbitrary"), vmem_limit_bytes=VMEM_LIMIT),
        name="fox",
    )(proj, proj, proj, proj, gates)


def _layer(x2d, batch, seq, ffn1_norm, ffn1_w_gate, ffn1_w_up, ffn1_w_down, mix_norm, w_in, b_in,
           conv_w, conv_b, ml_head_norm, w_out, ffn2_norm, ffn2_w_gate, ffn2_w_up, ffn2_w_down, final_g):
    d = D_MODEL
    x1, h1 = _ffn_call(x2d, ffn1_norm[None, :], ffn1_w_gate.astype(BF16), ffn1_w_up.astype(BF16),
                       ffn1_w_down.astype(BF16), next_g=mix_norm[None, :])

    sizes = (d, d, d, d, ML_HEADS, ML_HEADS, d, d, d, FOX_HEADS, d, d)
    starts = [0]
    for s in sizes[:-1]:
        starts.append(starts[-1] + s)
    (ml_q, ml_k, ml_v, ml_o, ml_i, ml_f, fx_q, fx_k, fx_v, fx_f, g_a, g_b) = [
        slice(a, a + s) for a, s in zip(starts, sizes)]
    order = [None] * N_COL_GROUPS
    for col, grp in ((COL_ML_Q, ml_q), (COL_ML_K, ml_k), (COL_ML_V, ml_v), (COL_ML_O, ml_o), (COL_G_A, g_a),
                     (COL_G_B, g_b), (COL_FX_Q, fx_q), (COL_FX_K, fx_k), (COL_FX_V, fx_v)):
        order[col] = grp
    w_t = w_in.T
    assert all(c.start % SUBLANES == 0 for c in order)
    w_main_t = _pack_w_call(w_t, jnp.asarray([c.start // SUBLANES for c in order], jnp.int32))
    b_main = jnp.concatenate([b_in[c] for c in order])[None, :]
    col_scale = jnp.ones((N_COL_GROUPS, d), F32).at[COL_FX_Q].set(LOG2E * FOX_HEAD_DIM ** -0.5).reshape(1, -1)
    pad = GATE_COLS - 2 * ML_HEADS - FOX_HEADS
    w_gates_t = jnp.concatenate([w_t[ml_i], w_t[ml_f], w_t[fx_f], jnp.zeros((pad, d), w_in.dtype)], axis=0)
    b_gates = jnp.concatenate([b_in[ml_i], b_in[ml_f], b_in[fx_f], jnp.zeros((pad,), b_in.dtype)])[None, :]
    proj, gates = _proj_call(h1, w_main_t, b_main, col_scale, w_gates_t, b_gates)

    z_a = _mlstm_call(proj, gates, conv_w, conv_b[None, :], ml_head_norm[None, :], batch, seq)
    z_b = _fox_call(proj, gates, batch, seq)

    return _ffn_call(x1, ffn2_norm[None, :], ffn2_w_gate.astype(BF16), ffn2_w_up.astype(BF16),
                     ffn2_w_down.astype(BF16), pre=(z_a, z_b, w_out.astype(BF16)), final_g=final_g)


def kernel(x, ffn1_norm, ffn1_w_gate, ffn1_w_up, ffn1_w_down, mix_norm, w_in, b_in, conv_w, conv_b,
           ml_head_norm, w_out, ffn2_norm, ffn2_w_gate, ffn2_w_up, ffn2_w_down, final_norm):
    batch, seq, d = x.shape
    assert d == D_MODEL and ffn1_norm.shape[0] == 1, "single-layer configuration"
    out = _layer(x.reshape(batch * seq, d), batch, seq, ffn1_norm[0], ffn1_w_gate[0], ffn1_w_up[0],
                 ffn1_w_down[0], mix_norm[0], w_in[0], b_in[0], conv_w[0], conv_b[0], ml_head_norm[0],
                 w_out[0], ffn2_norm[0], ffn2_w_gate[0], ffn2_w_up[0], ffn2_w_down[0], final_norm[None, :])
    return out.reshape(batch, seq, d)
```

```python
import functools
import math

import jax
import jax.numpy as jnp
from jax import lax
from jax.experimental import pallas as pl
from jax.experimental.pallas import tpu as pltpu

D_MODEL = 1024
ML_HEADS = 4
ML_HEAD_DIM = 256
CONV_WIDTH = 4
FOX_HEADS = 16
FOX_HEAD_DIM = 64
D_FF = 2816
EPS = 1e-6

LANES = 128
SUBLANES = 8
GATE_COLS = LANES
ML_I0, ML_F0, FX_F0 = 0, ML_HEADS, 2 * ML_HEADS

(COL_ML_Q, COL_ML_K, COL_ML_V, COL_ML_O, COL_G_A, COL_G_B, COL_FX_Q, COL_FX_K, COL_FX_V) = range(9)
N_COL_GROUPS = 9
LOG2E = 1.4426950408889634
K_SCALE_LOG = 0.5 * math.log(ML_HEAD_DIM)

FFN_TM = 1024
FFN_SUB = 512
FFN_TF = 256
PROJ_TM = 1024
PROJ_TN = 4608
ML_CHUNK = 256
HALO = 16
ML_PAD = 16
FOX_TQ = 256
FOX_TK = 2048
FOX_FEAT = 8
FOX_VROWS = 80
CS_BLK = 256
VMEM_LIMIT = 56 * 1024 * 1024

F32 = jnp.float32
BF16 = jnp.bfloat16


def _dot(a, b):
    return jnp.dot(a, b, preferred_element_type=F32)


def _dot_nt(a, b):
    return lax.dot_general(a, b, (((1,), (1,)), ((), ())), preferred_element_type=F32)


def _rmsnorm(x, g):
    return x * lax.rsqrt(jnp.mean(x * x, axis=-1, keepdims=True) + EPS) * g


def _sigmoid(x):
    return 1.0 / (1.0 + jnp.exp2(x * (-LOG2E)))


def _log_sigmoid(x):
    return jnp.minimum(x, 0.0) - jnp.log(1.0 + jnp.exp(-jnp.abs(x)))


def _cumsum_rows(x, tril):
    hi = x.astype(BF16)
    r1 = x - hi.astype(F32)
    mid = r1.astype(BF16)
    lo = (r1 - mid.astype(F32)).astype(BF16)
    return _dot(tril, hi) + _dot(tril, mid) + _dot(tril, lo)


def _tril(n):
    r = lax.broadcasted_iota(jnp.int32, (n, n), 0)
    c = lax.broadcasted_iota(jnp.int32, (n, n), 1)
    return r >= c


def _ffn_kernel(*refs, pre_proj, final_norm, next_norm):
    refs = list(refs)
    x_ref = refs.pop(0)
    if pre_proj:
        za_ref, zb_ref, wo_ref = refs.pop(0), refs.pop(0), refs.pop(0)
    g_ref, wg_ref, wu_ref, wd_ref = refs.pop(0), refs.pop(0), refs.pop(0), refs.pop(0)
    if final_norm:
        gf_ref = refs.pop(0)
    if next_norm:
        gn_ref = refs.pop(0)
    o_ref = refs.pop(0)
    if next_norm:
        hn_ref = refs.pop(0)
    (a_scr,) = refs

    def prologue(s):
        rows = slice(s * FFN_SUB, (s + 1) * FFN_SUB)
        x = x_ref[rows, :]
        if pre_proj:
            y = (za_ref[rows, :].astype(F32) + zb_ref[rows, :].astype(F32)).astype(BF16)
            x = x + _dot(y, wo_ref[...])
        return x, _rmsnorm(x, g_ref[...]).astype(BF16)

    def epilogue(s, x, y):
        rows = slice(s * FFN_SUB, (s + 1) * FFN_SUB)
        out = x + 0.5 * y
        if next_norm:
            hn_ref[rows, :] = _rmsnorm(out, gn_ref[...]).astype(BF16)
        if final_norm:
            out = _rmsnorm(out, gf_ref[...])
        o_ref[rows, :] = out

    n_sub = FFN_TM // FFN_SUB
    x, h = prologue(0)
    prev = None
    for s in range(n_sub):
        nxt = None
        for j in range(D_FF // FFN_TF):
            cols = slice(j * FFN_TF, (j + 1) * FFN_TF)
            gate = _dot(h, wg_ref[:, cols])
            up = _dot(h, wu_ref[:, cols])
            a_scr[s % 2, :, cols] = (gate * _sigmoid(gate) * up).astype(BF16)
            if j == 0 and prev is not None:
                prev = prev + (_dot(a_scr[(s - 1) % 2], wd_ref[...]),)
            if j == 1 and s + 1 < n_sub:
                nxt = prologue(s + 1)
            if j == 3 and prev is not None:
                epilogue(*prev)
                prev = None
        prev = (s, x)
        if nxt is not None:
            x, h = nxt
    epilogue(*prev, _dot(a_scr[(n_sub - 1) % 2], wd_ref[...]))


def _ffn_call(x, norm_g, wg, wu, wd, pre=None, final_g=None, next_g=None):
    n = x.shape[0]
    row = pl.BlockSpec((FFN_TM, D_MODEL), lambda i: (i, 0))
    vec = pl.BlockSpec((1, D_MODEL), lambda i: (0, 0))
    out_specs, out_shape = row, jax.ShapeDtypeStruct((n, D_MODEL), F32)
    if next_g is not None:
        out_specs, out_shape = [row, row], [out_shape, jax.ShapeDtypeStruct((n, D_MODEL), BF16)]

    def resident(shape):
        return pl.BlockSpec(shape, lambda i: (0, 0), pipeline_mode=pl.Buffered(1))

    args, specs = [x], [row]
    if pre is not None:
        za, zb, wo = pre
        args += [za, zb, wo]
        specs += [row, row, resident((D_MODEL, D_MODEL))]
    args += [norm_g, wg, wu, wd]
    specs += [vec, resident((D_MODEL, D_FF)), resident((D_MODEL, D_FF)), resident((D_FF, D_MODEL))]
    for g in (final_g, next_g):
        if g is not None:
            args.append(g)
            specs.append(vec)
    return pl.pallas_call(
        functools.partial(_ffn_kernel, pre_proj=pre is not None, final_norm=final_g is not None,
                          next_norm=next_g is not None),
        grid=(n // FFN_TM,),
        in_specs=specs,
        out_specs=out_specs,
        out_shape=out_shape,
        scratch_shapes=[pltpu.VMEM((2, FFN_SUB, D_FF), BF16)],
        compiler_params=pltpu.CompilerParams(
            dimension_semantics=("parallel",), vmem_limit_bytes=VMEM_LIMIT),
        name="ffn2" if pre is not None else "ffn1",
    )(*args)


def _pack_w_kernel(starts_ref, w_ref, o_ref):
    del starts_ref
    o_ref[...] = w_ref[...].astype(BF16)


def _pack_w_call(w_t, starts):
    n_groups = starts.shape[0]
    return pl.pallas_call(
        _pack_w_kernel,
        grid_spec=pltpu.PrefetchScalarGridSpec(
            num_scalar_prefetch=1,
            grid=(n_groups,),
            in_specs=[pl.BlockSpec((pl.Element(D_MODEL), pl.Element(D_MODEL)), lambda g, s: (s[g] * SUBLANES, 0))],
            out_specs=pl.BlockSpec((D_MODEL, D_MODEL), lambda g, s: (g, 0)),
        ),
        out_shape=jax.ShapeDtypeStruct((n_groups * D_MODEL, D_MODEL), BF16),
        compiler_params=pltpu.CompilerParams(dimension_semantics=("parallel",)),
        name="pack_w_in",
    )(starts, w_t)


def _proj_kernel(h_ref, w_ref, b_ref, s_ref, wgt_ref, bgt_ref, p_ref, gt_ref):
    h = h_ref[...]

    @pl.when(pl.program_id(1) == 0)
    def _():
        gt_ref[...] = _dot_nt(h, wgt_ref[...].astype(BF16)) + bgt_ref[...]

    p_ref[...] = ((_dot_nt(h, w_ref[...]) + b_ref[...]) * s_ref[...]).astype(BF16)


def _proj_call(h, w_t, b, col_scale, w_gates_t, b_gates):
    n = h.shape[0]
    n_out = w_t.shape[0]
    return pl.pallas_call(
        _proj_kernel,
        grid=(n // PROJ_TM, n_out // PROJ_TN),
        in_specs=[
            pl.BlockSpec((PROJ_TM, D_MODEL), lambda i, j: (i, 0)),
            pl.BlockSpec((PROJ_TN, D_MODEL), lambda i, j: (j, 0)),
            pl.BlockSpec((1, PROJ_TN), lambda i, j: (0, j)),
            pl.BlockSpec((1, PROJ_TN), lambda i, j: (0, j)),
            pl.BlockSpec((GATE_COLS, D_MODEL), lambda i, j: (0, 0)),
            pl.BlockSpec((1, GATE_COLS), lambda i, j: (0, 0)),
        ],
        out_specs=[
            pl.BlockSpec((PROJ_TM, PROJ_TN), lambda i, j: (i, j)),
            pl.BlockSpec((PROJ_TM, GATE_COLS), lambda i, j: (i, 0)),
        ],
        out_shape=[
            jax.ShapeDtypeStruct((n, n_out), BF16),
            jax.ShapeDtypeStruct((n, GATE_COLS), F32),
        ],
        compiler_params=pltpu.CompilerParams(
            dimension_semantics=("parallel", "arbitrary"), vmem_limit_bytes=VMEM_LIMIT),
        name="in_proj",
    )(h, w_t, b, col_scale, w_gates_t, b_gates)


def _mlstm_kernel(*refs, cast_steps):
    n_cast = len(cast_steps)
    ml_in, refs = refs[:8], refs[8:]
    cast_in, refs = refs[:n_cast], refs[n_cast:]
    z_ref, refs = refs[0], refs[1:]
    cast_out, scratch = refs[:n_cast], refs[n_cast:]
    step = pl.program_id(0) * pl.num_programs(1) + pl.program_id(1)
    for src, dst, n_steps in zip(cast_in, cast_out, cast_steps):
        @pl.when(step < n_steps)
        def _(src=src, dst=dst):
            dst[...] = src[...].astype(BF16)
    _mlstm_chunk(*ml_in, z_ref, *scratch)


def _mlstm_chunk(qk_ref, o_ref, ga_ref, v_ref, gt_ref, cw_ref, cb_ref, ln_ref, z_ref,
                 halo_scr, c_scr, m_scr):
    L = ML_CHUNK
    dh = ML_HEAD_DIM

    @pl.when(pl.program_id(1) == 0)
    def _():
        halo_scr[...] = jnp.zeros_like(halo_scr)
        c_scr[...] = jnp.zeros_like(c_scr)
        m_scr[...] = jnp.zeros_like(m_scr)

    u16 = qk_ref[...]
    halo16 = halo_scr[...]
    t_out = lax.broadcasted_iota(jnp.int32, (L, L), 0)
    t_in = lax.broadcasted_iota(jnp.int32, (L, L), 1)
    h_out = lax.broadcasted_iota(jnp.int32, (HALO, HALO), 0)
    h_in = lax.broadcasted_iota(jnp.int32, (HALO, HALO), 1)
    y = cb_ref[...] + cw_ref[CONV_WIDTH - 1:CONV_WIDTH, :] * u16.astype(F32)
    head_fix = jnp.zeros((HALO, 2 * D_MODEL), F32)
    for d in range(1, CONV_WIDTH):
        w_d = cw_ref[CONV_WIDTH - 1 - d:CONV_WIDTH - d, :]
        y = y + w_d * _dot(jnp.where(t_in == t_out - d, 1.0, 0.0).astype(BF16), u16)
        head_fix = head_fix + w_d * _dot(jnp.where(h_in == h_out - d + HALO, 1.0, 0.0).astype(BF16), halo16)
    y = jnp.concatenate([y[0:HALO] + head_fix, y[HALO:]], axis=0)
    halo_scr[...] = u16[L - HALO:L, :]
    qk = y * _sigmoid(y)

    gates = gt_ref[...]
    tril = _tril(L)
    bcum = _cumsum_rows(_log_sigmoid(gates), tril.astype(BF16))
    gates_t = gates.T
    bcum_t = bcum.T
    pad_row = lax.broadcasted_iota(jnp.int32, (ML_PAD, L), 0)
    ones_pad = jnp.where(pad_row == 0, 1.0, 0.0).astype(BF16)

    def matmul_stage(h):
        q_t = qk[:, h * dh:(h + 1) * dh].astype(BF16).T
        k16 = qk[:, D_MODEL + h * dh:D_MODEL + (h + 1) * dh].astype(BF16)
        state = c_scr[h]
        return q_t, k16, state, _dot(k16, q_t), _dot(state.astype(BF16), q_t)

    staged = matmul_stage(0)
    for h in range(ML_HEADS):
        cols = slice(h * dh, (h + 1) * dh)
        q_t, k16, state, qk_t, inter_t = staged
        if h + 1 < ML_HEADS:
            staged = matmul_stage(h + 1)
        v_aug = jnp.concatenate([v_ref[:, cols].T, ones_pad], axis=0)
        li_row = gates_t[ML_I0 + h:ML_I0 + h + 1, :]
        b_row = bcum_t[ML_F0 + h:ML_F0 + h + 1, :]
        ib_col = gates[:, ML_I0 + h:ML_I0 + h + 1] - bcum[:, ML_F0 + h:ML_F0 + h + 1]
        m_prev = m_scr[h][:, 0:1]

        dmat_t = jnp.where(t_out <= t_in, b_row + ib_col, -jnp.inf)
        inter = b_row + m_prev
        m_t = jnp.maximum(jnp.max(dmat_t, axis=0, keepdims=True), inter)
        w_inter = jnp.exp(inter - m_t)
        s_t = (qk_t * jnp.exp(dmat_t - (m_t + K_SCALE_LOG))).astype(BF16)
        num_den = _dot(v_aug, s_t) + w_inter * inter_t
        den = num_den[dh:dh + 1, :]
        hh_t = num_den[0:dh, :] / jnp.maximum(jnp.abs(den), jnp.exp(-m_t))

        b_last = b_row[:, L - 1:L]
        log_w = b_last - b_row + li_row
        m_new = jnp.maximum(b_last + m_prev, jnp.max(log_w, axis=1, keepdims=True))
        decay = jnp.exp(b_last + m_prev - m_new)
        w = jnp.exp(log_w - m_new) * (dh ** -0.5)
        c_scr[h] = decay * state + _dot((v_aug.astype(F32) * w).astype(BF16), k16)
        m_scr[h] = jnp.broadcast_to(m_new, (1, LANES))

        mu = jnp.mean(hh_t, axis=0, keepdims=True)
        dlt = hh_t - mu
        var = jnp.mean(dlt * dlt, axis=0, keepdims=True)
        hn = (dlt * lax.rsqrt(var + EPS)).T * ln_ref[:, cols]
        gate = _sigmoid(ga_ref[:, cols].astype(F32)) * _sigmoid(o_ref[:, cols].astype(F32))
        z_ref[:, cols] = (gate * hn).astype(BF16)


def _mlstm_call(proj, gates, conv_w, conv_b, ln_g, batch, seq, side_casts):
    n = batch * seq
    nc = seq // ML_CHUNK
    L = ML_CHUNK
    total_steps = batch * nc

    def row(width, col):
        return pl.BlockSpec((L, width), lambda b, c: (b * nc + c, col))

    def full(shape):
        return pl.BlockSpec(shape, lambda b, c: (0, 0))

    cast_specs, cast_shapes, cast_steps = [], [], []
    for w in side_casts:
        rows, width = w.shape
        n_steps = max(s for s in range(1, total_steps + 1) if rows % s == 0 and (rows // s) % (2 * SUBLANES) == 0)
        spec = pl.BlockSpec((rows // n_steps, width), lambda b, c, last=n_steps - 1: (jnp.minimum(b * nc + c, last), 0))
        cast_specs.append(spec)
        cast_shapes.append(jax.ShapeDtypeStruct(w.shape, BF16))
        cast_steps.append(n_steps)

    z_a, *casted = pl.pallas_call(
        functools.partial(_mlstm_kernel, cast_steps=tuple(cast_steps)),
        grid=(batch, nc),
        in_specs=[
            row(2 * D_MODEL, COL_ML_Q // 2),
            row(D_MODEL, COL_ML_O),
            row(D_MODEL, COL_G_A),
            row(D_MODEL, COL_ML_V),
            row(GATE_COLS, 0),
            full((CONV_WIDTH, 2 * D_MODEL)),
            full((1, 2 * D_MODEL)),
            full((1, D_MODEL)),
            *cast_specs,
        ],
        out_specs=[row(D_MODEL, 0), *cast_specs],
        out_shape=[jax.ShapeDtypeStruct((n, D_MODEL), BF16), *cast_shapes],
        scratch_shapes=[
            pltpu.VMEM((HALO, 2 * D_MODEL), BF16),
            pltpu.VMEM((ML_HEADS, ML_HEAD_DIM + ML_PAD, ML_HEAD_DIM), F32),
            pltpu.VMEM((ML_HEADS, 1, LANES), F32),
        ],
        compiler_params=pltpu.CompilerParams(
            dimension_semantics=("arbitrary", "arbitrary"), vmem_limit_bytes=VMEM_LIMIT),
        name="mlstm",
    )(proj, proj, proj, proj, gates, conv_w, conv_b, ln_g, *side_casts)
    return z_a, casted


def _fox_bias_features(gt_ref, qf_scr, ka_scr, seq):
    tril = _tril(CS_BLK).astype(BF16)
    r = lax.broadcasted_iota(jnp.int32, (GATE_COLS, 2 * LANES), 0)
    c = lax.broadcasted_iota(jnp.int32, (GATE_COLS, 2 * LANES), 1)
    head = r - FX_F0
    is_head = (head >= 0) & (head < FOX_HEADS)
    sel = []
    for piece in range(3):
        q_hit = is_head & (c == FOX_FEAT * head + piece)
        k_hit = is_head & (c == LANES + FOX_FEAT * head + 3 + piece)
        sel.append(jnp.where(q_hit, 1.0, jnp.where(k_hit, -1.0, 0.0)).astype(BF16))
    lane = lax.broadcasted_iota(jnp.int32, (1, 2 * LANES), 1)
    slot = lane % FOX_FEAT
    in_use = (lane % LANES) < FOX_FEAT * FOX_HEADS
    q_one = (lane < LANES) & (slot >= 3) & (slot < 6)
    k_one = (lane >= LANES) & (slot < 3)
    ones = jnp.where(in_use & (q_one | k_one), 1.0, 0.0)

    carry = jnp.zeros((1, GATE_COLS), F32)
    for blk in range(seq // CS_BLK):
        rows = slice(blk * CS_BLK, (blk + 1) * CS_BLK)
        cs = _cumsum_rows(_log_sigmoid(gt_ref[rows, :]), tril) + carry
        carry = cs[CS_BLK - 1:CS_BLK, :]
        cs2 = cs * LOG2E
        hi = cs2.astype(BF16)
        r1 = cs2 - hi.astype(F32)
        mid = r1.astype(BF16)
        lo = (r1 - mid.astype(F32)).astype(BF16)
        feat = (_dot(hi, sel[0]) + _dot(mid, sel[1]) + _dot(lo, sel[2]) + ones).astype(BF16)
        qf_scr[rows, :] = feat[:, 0:LANES]
        ka_scr[rows, LANES:2 * LANES] = feat[:, LANES:2 * LANES]


def _fox_kernel(q_ref, k_ref, v_ref, gb_ref, gt_ref, z_ref, qf_scr, ka_scr, qt_scr, vt_scr, tri_scr, *, seq):
    tq = FOX_TQ
    hd = FOX_HEAD_DIM
    pair = pl.program_id(1)

    @pl.when(pair == 0)
    def _():
        _fox_bias_features(gt_ref, qf_scr, ka_scr, seq)
        key = lax.broadcasted_iota(jnp.int32, (tq, tq), 0)
        qry = lax.broadcasted_iota(jnp.int32, (tq, tq), 1)
        tri_scr[...] = jnp.where(qry >= key, 0.0, -jnp.inf)
        one_row = lax.broadcasted_iota(jnp.int32, (FOX_VROWS - hd, seq), 0) == 0
        for e in range(2):
            vt_scr[e, hd:FOX_VROWS, :] = jnp.where(one_row, 1.0, 0.0).astype(BF16)

    ka_scr[:, 0:LANES] = k_ref[...]
    sub = lax.broadcasted_iota(jnp.int32, (2 * LANES, 1), 0)
    keep = []
    for e in range(2):
        feat0 = LANES + FOX_FEAT * (2 * pair + e)
        own_q = (sub >= e * FOX_HEAD_DIM) & (sub < (e + 1) * FOX_HEAD_DIM)
        own_feat = (sub >= feat0) & (sub < feat0 + FOX_FEAT)
        keep.append(jnp.where(own_q | own_feat, 1.0, 0.0).astype(BF16))
    for blk in range(seq // tq):
        rows = slice(blk * tq, (blk + 1) * tq)
        q_aug_t = jnp.concatenate([q_ref[rows, :], qf_scr[rows, :]], axis=1).T
        v_t = v_ref[rows, :].T
        for e in range(2):
            qt_scr[e, :, rows] = q_aug_t * keep[e]
            vt_scr[e, 0:hd, rows] = v_t[e * hd:(e + 1) * hd, :]

    items = []
    for qi in range(seq // tq):
        q0 = qi * tq
        if FOX_TK <= tq:
            spans = [(k0, FOX_TK) for k0 in range(0, q0 + tq, FOX_TK)]
        else:
            spans = [(k0, FOX_TK) for k0 in range(0, q0 - FOX_TK + 1, FOX_TK)]
            k_done = len(spans) * FOX_TK
            spans.append((k_done, q0 + tq - k_done))
        for si, (k0, width) in enumerate(spans):
            for e in range(2):
                items.append((q0, k0, width, k0 + width > q0, e, si == 0, si == len(spans) - 1))
    n_items = len(items)

    def logits_of(idx):
        q0, k0, width, _, e, _, _ = items[idx]
        return _dot(ka_scr[k0:k0 + width, :], qt_scr[e, :, q0:q0 + tq])

    m = {}

    def softmax_of(idx, logit_t):
        q0, k0, width, masked, e, first, _ = items[idx]
        if masked:
            lo = max(q0 - k0, 0)
            diag = logit_t[lo:] + tri_scr[k0 + lo - q0:k0 + width - q0, :]
            logit_t = diag if lo == 0 else jnp.concatenate([logit_t[:lo], diag], axis=0)
        m_blk = jnp.max(logit_t, axis=0, keepdims=True)
        m_new = m_blk if first else jnp.maximum(m[e], m_blk)
        alpha = None if first else jnp.exp2(m[e] - m_new)
        m[e] = m_new
        return alpha, jnp.exp2(logit_t - m_new).astype(BF16)

    acc = {}
    logits = {i: logits_of(i) for i in range(min(2, n_items))}
    probs = {0: softmax_of(0, logits.pop(0))}
    for idx, (q0, k0, width, masked, e, first, last) in enumerate(items):
        if idx + 2 < n_items:
            logits[idx + 2] = logits_of(idx + 2)
        if idx + 1 < n_items:
            probs[idx + 1] = softmax_of(idx + 1, logits.pop(idx + 1))
        alpha, p_t = probs.pop(idx)
        pv = _dot(vt_scr[e, :, k0:k0 + width], p_t)
        acc[e] = pv if first else alpha * acc[e] + pv
        if last and e == 1:
            o_t = jnp.concatenate([acc[h][0:hd] / acc[h][hd:hd + 1] for h in range(2)], axis=0)
            z_ref[q0:q0 + tq, :] = (_sigmoid(gb_ref[q0:q0 + tq, :].astype(F32)) * o_t.T).astype(BF16)


def _fox_call(proj, gates, batch, seq):
    n = batch * seq
    pairs = FOX_HEADS // 2
    per_group = D_MODEL // LANES

    def cols(group):
        return pl.BlockSpec((seq, LANES), lambda b, p: (b, group * per_group + p))

    return pl.pallas_call(
        functools.partial(_fox_kernel, seq=seq),
        grid=(batch, pairs),
        in_specs=[
            cols(COL_FX_Q),
            cols(COL_FX_K),
            cols(COL_FX_V),
            cols(COL_G_B),
            pl.BlockSpec((seq, GATE_COLS), lambda b, p: (b, 0)),
        ],
        out_specs=pl.BlockSpec((seq, LANES), lambda b, p: (b, p)),
        out_shape=jax.ShapeDtypeStruct((n, D_MODEL), BF16),
        scratch_shapes=[
            pltpu.VMEM((seq, LANES), BF16),
            pltpu.VMEM((seq, 2 * LANES), BF16),
            pltpu.VMEM((2, 2 * LANES, seq), BF16),
            pltpu.VMEM((2, FOX_VROWS, seq), BF16),
            pltpu.VMEM((FOX_TQ, FOX_TQ), F32),
        ],
        compiler_params=pltpu.CompilerParams(
            dimension_semantics=("parallel", "arbitrary"), vmem_limit_bytes=VMEM_LIMIT),
        name="fox",
    )(proj, proj, proj, proj, gates)


def _layer(x2d, batch, seq, ffn1_norm, ffn1_w_gate, ffn1_w_up, ffn1_w_down, mix_norm, w_in, b_in,
           conv_w, conv_b, ml_head_norm, w_out, ffn2_norm, ffn2_w_gate, ffn2_w_up, ffn2_w_down, final_g):
    d = D_MODEL
    x1, h1 = _ffn_call(x2d, ffn1_norm[None, :], ffn1_w_gate.astype(BF16), ffn1_w_up.astype(BF16),
                       ffn1_w_down.astype(BF16), next_g=mix_norm[None, :])

    sizes = (d, d, d, d, ML_HEADS, ML_HEADS, d, d, d, FOX_HEADS, d, d)
    starts = [0]
    for s in sizes[:-1]:
        starts.append(starts[-1] + s)
    (ml_q, ml_k, ml_v, ml_o, ml_i, ml_f, fx_q, fx_k, fx_v, fx_f, g_a, g_b) = [
        slice(a, a + s) for a, s in zip(starts, sizes)]
    order = [None] * N_COL_GROUPS
    for col, grp in ((COL_ML_Q, ml_q), (COL_ML_K, ml_k), (COL_ML_V, ml_v), (COL_ML_O, ml_o), (COL_G_A, g_a),
                     (COL_G_B, g_b), (COL_FX_Q, fx_q), (COL_FX_K, fx_k), (COL_FX_V, fx_v)):
        order[col] = grp
    w_t = w_in.T
    assert all(c.start % SUBLANES == 0 for c in order)
    w_main_t = _pack_w_call(w_t, jnp.asarray([c.start // SUBLANES for c in order], jnp.int32))
    b_main = jnp.concatenate([b_in[c] for c in order])[None, :]
    col_scale = jnp.ones((N_COL_GROUPS, d), F32).at[COL_FX_Q].set(LOG2E * FOX_HEAD_DIM ** -0.5).reshape(1, -1)
    pad = GATE_COLS - 2 * ML_HEADS - FOX_HEADS
    w_gates_t = jnp.concatenate([w_t[ml_i], w_t[ml_f], w_t[fx_f], jnp.zeros((pad, d), w_in.dtype)], axis=0)
    b_gates = jnp.concatenate([b_in[ml_i], b_in[ml_f], b_in[fx_f], jnp.zeros((pad,), b_in.dtype)])[None, :]
    proj, gates = _proj_call(h1, w_main_t, b_main, col_scale, w_gates_t, b_gates)

    z_a, (wg2, wu2, wd2, wo) = _mlstm_call(proj, gates, conv_w, conv_b[None, :], ml_head_norm[None, :], batch, seq,
                                           side_casts=(ffn2_w_gate, ffn2_w_up, ffn2_w_down, w_out))
    z_b = _fox_call(proj, gates, batch, seq)

    return _ffn_call(x1, ffn2_norm[None, :], wg2, wu2, wd2, pre=(z_a, z_b, wo), final_g=final_g)


def kernel(x, ffn1_norm, ffn1_w_gate, ffn1_w_up, ffn1_w_down, mix_norm, w_in, b_in, conv_w, conv_b,
           ml_head_norm, w_out, ffn2_norm, ffn2_w_gate, ffn2_w_up, ffn2_w_down, final_norm):
    batch, seq, d = x.shape
    assert d == D_MODEL and ffn1_norm.shape[0] == 1, "single-layer configuration"
    out = _layer(x.reshape(batch * seq, d), batch, seq, ffn1_norm[0], ffn1_w_gate[0], ffn1_w_up[0],
                 ffn1_w_down[0], mix_norm[0], w_in[0], b_in[0], conv_w[0], conv_b[0], ml_head_norm[0],
                 w_out[0], ffn2_norm[0], ffn2_w_gate[0], ffn2_w_up[0], ffn2_w_down[0], final_norm[None, :])
    return out.reshape(batch, seq, d)
```

```python
import functools
import math

import jax
import jax.numpy as jnp
from jax import lax
from jax.experimental import pallas as pl
from jax.experimental.pallas import tpu as pltpu

D_MODEL = 1024
ML_HEADS = 4
ML_HEAD_DIM = 256
CONV_WIDTH = 4
FOX_HEADS = 16
FOX_HEAD_DIM = 64
D_FF = 2816
EPS = 1e-6

LANES = 128
SUBLANES = 8
GATE_COLS = LANES
ML_I0, ML_F0, FX_F0 = 0, ML_HEADS, 2 * ML_HEADS

(COL_ML_Q, COL_ML_K, COL_ML_V, COL_ML_O, COL_G_A, COL_G_B, COL_FX_Q, COL_FX_K, COL_FX_V) = range(9)
N_COL_GROUPS = 9
LOG2E = 1.4426950408889634
K_SCALE_LOG = 0.5 * math.log(ML_HEAD_DIM)

FFN_TM = 1024
FFN_SUB = 512
FFN_TF = 256
PROJ_TM = 1024
PROJ_TN = 4608
ML_CHUNK = 256
HALO = 16
ML_PAD = 16
FOX_TQ = 256
FOX_TK = 2048
FOX_FEAT = 8
FOX_VROWS = 80
VMEM_LIMIT = 56 * 1024 * 1024

F32 = jnp.float32
BF16 = jnp.bfloat16


def _dot(a, b):
    return jnp.dot(a, b, preferred_element_type=F32)


def _dot_nt(a, b):
    return lax.dot_general(a, b, (((1,), (1,)), ((), ())), preferred_element_type=F32)


def _rmsnorm(x, g):
    return x * lax.rsqrt(jnp.mean(x * x, axis=-1, keepdims=True) + EPS) * g


def _sigmoid(x):
    return 1.0 / (1.0 + jnp.exp2(x * (-LOG2E)))


def _log_sigmoid(x):
    return jnp.minimum(x, 0.0) - jnp.log(1.0 + jnp.exp(-jnp.abs(x)))


def _cumsum_rows(x, tril):
    hi = x.astype(BF16)
    r1 = x - hi.astype(F32)
    mid = r1.astype(BF16)
    lo = (r1 - mid.astype(F32)).astype(BF16)
    return _dot(tril, hi) + _dot(tril, mid) + _dot(tril, lo)


def _tril(n):
    r = lax.broadcasted_iota(jnp.int32, (n, n), 0)
    c = lax.broadcasted_iota(jnp.int32, (n, n), 1)
    return r >= c


def _ffn_kernel(*refs, pre_proj, final_norm, next_norm):
    refs = list(refs)
    x_ref = refs.pop(0)
    if pre_proj:
        za_ref, zb_ref, wo_ref = refs.pop(0), refs.pop(0), refs.pop(0)
    g_ref, wg_ref, wu_ref, wd_ref = refs.pop(0), refs.pop(0), refs.pop(0), refs.pop(0)
    if final_norm:
        gf_ref = refs.pop(0)
    if next_norm:
        gn_ref = refs.pop(0)
    o_ref = refs.pop(0)
    if next_norm:
        hn_ref = refs.pop(0)
    (a_scr,) = refs

    def prologue(s):
        rows = slice(s * FFN_SUB, (s + 1) * FFN_SUB)
        x = x_ref[rows, :]
        if pre_proj:
            y = (za_ref[rows, :].astype(F32) + zb_ref[rows, :].astype(F32)).astype(BF16)
            x = x + _dot(y, wo_ref[...])
        return x, _rmsnorm(x, g_ref[...]).astype(BF16)

    def epilogue(s, x, y):
        rows = slice(s * FFN_SUB, (s + 1) * FFN_SUB)
        out = x + 0.5 * y
        if next_norm:
            hn_ref[rows, :] = _rmsnorm(out, gn_ref[...]).astype(BF16)
        if final_norm:
            out = _rmsnorm(out, gf_ref[...])
        o_ref[rows, :] = out

    n_sub = FFN_TM // FFN_SUB
    x, h = prologue(0)
    prev = None
    for s in range(n_sub):
        nxt = None
        for j in range(D_FF // FFN_TF):
            cols = slice(j * FFN_TF, (j + 1) * FFN_TF)
            gate = _dot(h, wg_ref[:, cols])
            up = _dot(h, wu_ref[:, cols])
            a_scr[s % 2, :, cols] = (gate * _sigmoid(gate) * up).astype(BF16)
            if j == 0 and prev is not None:
                prev = prev + (_dot(a_scr[(s - 1) % 2], wd_ref[...]),)
            if j == 1 and s + 1 < n_sub:
                nxt = prologue(s + 1)
            if j == 3 and prev is not None:
                epilogue(*prev)
                prev = None
        prev = (s, x)
        if nxt is not None:
            x, h = nxt
    epilogue(*prev, _dot(a_scr[(n_sub - 1) % 2], wd_ref[...]))


def _ffn_call(x, norm_g, wg, wu, wd, pre=None, final_g=None, next_g=None):
    n = x.shape[0]
    row = pl.BlockSpec((FFN_TM, D_MODEL), lambda i: (i, 0))
    vec = pl.BlockSpec((1, D_MODEL), lambda i: (0, 0))
    out_specs, out_shape = row, jax.ShapeDtypeStruct((n, D_MODEL), F32)
    if next_g is not None:
        out_specs, out_shape = [row, row], [out_shape, jax.ShapeDtypeStruct((n, D_MODEL), BF16)]

    def resident(shape):
        return pl.BlockSpec(shape, lambda i: (0, 0), pipeline_mode=pl.Buffered(1))

    args, specs = [x], [row]
    if pre is not None:
        za, zb, wo = pre
        args += [za, zb, wo]
        specs += [row, row, resident((D_MODEL, D_MODEL))]
    args += [norm_g, wg, wu, wd]
    specs += [vec, resident((D_MODEL, D_FF)), resident((D_MODEL, D_FF)), resident((D_FF, D_MODEL))]
    for g in (final_g, next_g):
        if g is not None:
            args.append(g)
            specs.append(vec)
    return pl.pallas_call(
        functools.partial(_ffn_kernel, pre_proj=pre is not None, final_norm=final_g is not None,
                          next_norm=next_g is not None),
        grid=(n // FFN_TM,),
        in_specs=specs,
        out_specs=out_specs,
        out_shape=out_shape,
        scratch_shapes=[pltpu.VMEM((2, FFN_SUB, D_FF), BF16)],
        compiler_params=pltpu.CompilerParams(
            dimension_semantics=("parallel",), vmem_limit_bytes=VMEM_LIMIT),
        name="ffn2" if pre is not None else "ffn1",
    )(*args)


def _pack_w_kernel(starts_ref, w_ref, o_ref):
    del starts_ref
    o_ref[...] = w_ref[...].astype(BF16)


def _pack_w_call(w_t, starts):
    n_groups = starts.shape[0]
    return pl.pallas_call(
        _pack_w_kernel,
        grid_spec=pltpu.PrefetchScalarGridSpec(
            num_scalar_prefetch=1,
            grid=(n_groups,),
            in_specs=[pl.BlockSpec((pl.Element(D_MODEL), pl.Element(D_MODEL)), lambda g, s: (s[g] * SUBLANES, 0))],
            out_specs=pl.BlockSpec((D_MODEL, D_MODEL), lambda g, s: (g, 0)),
        ),
        out_shape=jax.ShapeDtypeStruct((n_groups * D_MODEL, D_MODEL), BF16),
        compiler_params=pltpu.CompilerParams(dimension_semantics=("parallel",)),
        name="pack_w_in",
    )(starts, w_t)


def _proj_kernel(h_ref, w_ref, b_ref, s_ref, wgt_ref, bgt_ref, p_ref, gt_ref):
    h = h_ref[...]

    @pl.when(pl.program_id(1) == 0)
    def _():
        gt_ref[...] = _dot_nt(h, wgt_ref[...].astype(BF16)) + bgt_ref[...]

    p_ref[...] = ((_dot_nt(h, w_ref[...]) + b_ref[...]) * s_ref[...]).astype(BF16)


def _proj_call(h, w_t, b, col_scale, w_gates_t, b_gates):
    n = h.shape[0]
    n_out = w_t.shape[0]
    return pl.pallas_call(
        _proj_kernel,
        grid=(n // PROJ_TM, n_out // PROJ_TN),
        in_specs=[
            pl.BlockSpec((PROJ_TM, D_MODEL), lambda i, j: (i, 0)),
            pl.BlockSpec((PROJ_TN, D_MODEL), lambda i, j: (j, 0)),
            pl.BlockSpec((1, PROJ_TN), lambda i, j: (0, j)),
            pl.BlockSpec((1, PROJ_TN), lambda i, j: (0, j)),
            pl.BlockSpec((GATE_COLS, D_MODEL), lambda i, j: (0, 0)),
            pl.BlockSpec((1, GATE_COLS), lambda i, j: (0, 0)),
        ],
        out_specs=[
            pl.BlockSpec((PROJ_TM, PROJ_TN), lambda i, j: (i, j)),
            pl.BlockSpec((PROJ_TM, GATE_COLS), lambda i, j: (i, 0)),
        ],
        out_shape=[
            jax.ShapeDtypeStruct((n, n_out), BF16),
            jax.ShapeDtypeStruct((n, GATE_COLS), F32),
        ],
        compiler_params=pltpu.CompilerParams(
            dimension_semantics=("parallel", "arbitrary"), vmem_limit_bytes=VMEM_LIMIT),
        name="in_proj",
    )(h, w_t, b, col_scale, w_gates_t, b_gates)


def _mlstm_kernel(*refs, cast_steps):
    n_cast = len(cast_steps)
    ml_in, refs = refs[:8], refs[8:]
    cast_in, refs = refs[:n_cast], refs[n_cast:]
    ml_out, refs = refs[:3], refs[3:]
    cast_out, scratch = refs[:n_cast], refs[n_cast:]
    step = pl.program_id(0) * pl.num_programs(1) + pl.program_id(1)
    for src, dst, n_steps in zip(cast_in, cast_out, cast_steps):
        @pl.when(step < n_steps)
        def _(src=src, dst=dst):
            dst[...] = src[...].astype(BF16)
    _mlstm_chunk(*ml_in, *ml_out, *scratch)


def _mlstm_chunk(qk_ref, o_ref, ga_ref, v_ref, gt_ref, cw_ref, cb_ref, ln_ref, z_ref, qf_ref, kf_ref,
                 halo_scr, c_scr, m_scr, cum_scr):
    L = ML_CHUNK
    dh = ML_HEAD_DIM

    @pl.when(pl.program_id(1) == 0)
    def _():
        halo_scr[...] = jnp.zeros_like(halo_scr)
        c_scr[...] = jnp.zeros_like(c_scr)
        m_scr[...] = jnp.zeros_like(m_scr)
        cum_scr[...] = jnp.zeros_like(cum_scr)

    u16 = qk_ref[...]
    halo16 = halo_scr[...]
    t_out = lax.broadcasted_iota(jnp.int32, (L, L), 0)
    t_in = lax.broadcasted_iota(jnp.int32, (L, L), 1)
    h_out = lax.broadcasted_iota(jnp.int32, (HALO, HALO), 0)
    h_in = lax.broadcasted_iota(jnp.int32, (HALO, HALO), 1)
    y = cb_ref[...] + cw_ref[CONV_WIDTH - 1:CONV_WIDTH, :] * u16.astype(F32)
    head_fix = jnp.zeros((HALO, 2 * D_MODEL), F32)
    for d in range(1, CONV_WIDTH):
        w_d = cw_ref[CONV_WIDTH - 1 - d:CONV_WIDTH - d, :]
        y = y + w_d * _dot(jnp.where(t_in == t_out - d, 1.0, 0.0).astype(BF16), u16)
        head_fix = head_fix + w_d * _dot(jnp.where(h_in == h_out - d + HALO, 1.0, 0.0).astype(BF16), halo16)
    y = jnp.concatenate([y[0:HALO] + head_fix, y[HALO:]], axis=0)
    halo_scr[...] = u16[L - HALO:L, :]
    qk = y * _sigmoid(y)

    gates = gt_ref[...]
    tril = _tril(L)
    bcum = _cumsum_rows(_log_sigmoid(gates), tril.astype(BF16))
    gates_t = gates.T
    bcum_t = bcum.T
    cum = bcum + cum_scr[...]
    cum_scr[...] = cum[L - 1:L, :]
    qf_ref[...], kf_ref[...] = _fox_bias_features(cum)
    pad_row = lax.broadcasted_iota(jnp.int32, (ML_PAD, L), 0)
    ones_pad = jnp.where(pad_row == 0, 1.0, 0.0).astype(BF16)

    def matmul_stage(h):
        q_t = qk[:, h * dh:(h + 1) * dh].astype(BF16).T
        k16 = qk[:, D_MODEL + h * dh:D_MODEL + (h + 1) * dh].astype(BF16)
        state = c_scr[h]
        return q_t, k16, state, _dot(k16, q_t), _dot(state.astype(BF16), q_t)

    staged = matmul_stage(0)
    for h in range(ML_HEADS):
        cols = slice(h * dh, (h + 1) * dh)
        q_t, k16, state, qk_t, inter_t = staged
        if h + 1 < ML_HEADS:
            staged = matmul_stage(h + 1)
        v_aug = jnp.concatenate([v_ref[:, cols].T, ones_pad], axis=0)
        li_row = gates_t[ML_I0 + h:ML_I0 + h + 1, :]
        b_row = bcum_t[ML_F0 + h:ML_F0 + h + 1, :]
        ib_col = gates[:, ML_I0 + h:ML_I0 + h + 1] - bcum[:, ML_F0 + h:ML_F0 + h + 1]
        m_prev = m_scr[h][:, 0:1]

        dmat_t = jnp.where(t_out <= t_in, b_row + ib_col, -jnp.inf)
        inter = b_row + m_prev
        m_t = jnp.maximum(jnp.max(dmat_t, axis=0, keepdims=True), inter)
        w_inter = jnp.exp(inter - m_t)
        s_t = (qk_t * jnp.exp(dmat_t - (m_t + K_SCALE_LOG))).astype(BF16)
        num_den = _dot(v_aug, s_t) + w_inter * inter_t
        den = num_den[dh:dh + 1, :]
        hh_t = num_den[0:dh, :] / jnp.maximum(jnp.abs(den), jnp.exp(-m_t))

        b_last = b_row[:, L - 1:L]
        log_w = b_last - b_row + li_row
        m_new = jnp.maximum(b_last + m_prev, jnp.max(log_w, axis=1, keepdims=True))
        decay = jnp.exp(b_last + m_prev - m_new)
        w = jnp.exp(log_w - m_new) * (dh ** -0.5)
        c_scr[h] = decay * state + _dot((v_aug.astype(F32) * w).astype(BF16), k16)
        m_scr[h] = jnp.broadcast_to(m_new, (1, LANES))

        mu = jnp.mean(hh_t, axis=0, keepdims=True)
        dlt = hh_t - mu
        var = jnp.mean(dlt * dlt, axis=0, keepdims=True)
        hn = (dlt * lax.rsqrt(var + EPS)).T * ln_ref[:, cols]
        gate = _sigmoid(ga_ref[:, cols].astype(F32)) * _sigmoid(o_ref[:, cols].astype(F32))
        z_ref[:, cols] = (gate * hn).astype(BF16)


def _mlstm_call(proj, gates, conv_w, conv_b, ln_g, batch, seq, side_casts):
    n = batch * seq
    nc = seq // ML_CHUNK
    L = ML_CHUNK
    total_steps = batch * nc

    def row(width, col):
        return pl.BlockSpec((L, width), lambda b, c: (b * nc + c, col))

    def full(shape):
        return pl.BlockSpec(shape, lambda b, c: (0, 0))

    cast_specs, cast_shapes, cast_steps = [], [], []
    for w in side_casts:
        rows, width = w.shape
        n_steps = max(s for s in range(1, total_steps + 1) if rows % s == 0 and (rows // s) % (2 * SUBLANES) == 0)
        spec = pl.BlockSpec((rows // n_steps, width), lambda b, c, last=n_steps - 1: (jnp.minimum(b * nc + c, last), 0))
        cast_specs.append(spec)
        cast_shapes.append(jax.ShapeDtypeStruct(w.shape, BF16))
        cast_steps.append(n_steps)

    feat_shape = jax.ShapeDtypeStruct((n, LANES), BF16)
    z_a, q_feat, k_feat, *casted = pl.pallas_call(
        functools.partial(_mlstm_kernel, cast_steps=tuple(cast_steps)),
        grid=(batch, nc),
        in_specs=[
            row(2 * D_MODEL, COL_ML_Q // 2),
            row(D_MODEL, COL_ML_O),
            row(D_MODEL, COL_G_A),
            row(D_MODEL, COL_ML_V),
            row(GATE_COLS, 0),
            full((CONV_WIDTH, 2 * D_MODEL)),
            full((1, 2 * D_MODEL)),
            full((1, D_MODEL)),
            *cast_specs,
        ],
        out_specs=[row(D_MODEL, 0), row(LANES, 0), row(LANES, 0), *cast_specs],
        out_shape=[jax.ShapeDtypeStruct((n, D_MODEL), BF16), feat_shape, feat_shape, *cast_shapes],
        scratch_shapes=[
            pltpu.VMEM((HALO, 2 * D_MODEL), BF16),
            pltpu.VMEM((ML_HEADS, ML_HEAD_DIM + ML_PAD, ML_HEAD_DIM), F32),
            pltpu.VMEM((ML_HEADS, 1, LANES), F32),
            pltpu.VMEM((1, GATE_COLS), F32),
        ],
        compiler_params=pltpu.CompilerParams(
            dimension_semantics=("arbitrary", "arbitrary"), vmem_limit_bytes=VMEM_LIMIT),
        name="mlstm",
    )(proj, proj, proj, proj, gates, conv_w, conv_b, ln_g, *side_casts)
    return z_a, q_feat, k_feat, casted


def _fox_bias_features(cs):
    r = lax.broadcasted_iota(jnp.int32, (GATE_COLS, 2 * LANES), 0)
    c = lax.broadcasted_iota(jnp.int32, (GATE_COLS, 2 * LANES), 1)
    head = r - FX_F0
    is_head = (head >= 0) & (head < FOX_HEADS)
    sel = []
    for piece in range(3):
        q_hit = is_head & (c == FOX_FEAT * head + piece)
        k_hit = is_head & (c == LANES + FOX_FEAT * head + 3 + piece)
        sel.append(jnp.where(q_hit, 1.0, jnp.where(k_hit, -1.0, 0.0)).astype(BF16))
    lane = lax.broadcasted_iota(jnp.int32, (1, 2 * LANES), 1)
    slot = lane % FOX_FEAT
    in_use = (lane % LANES) < FOX_FEAT * FOX_HEADS
    q_one = (lane < LANES) & (slot >= 3) & (slot < 6)
    k_one = (lane >= LANES) & (slot < 3)
    ones = jnp.where(in_use & (q_one | k_one), 1.0, 0.0)

    cs2 = cs * LOG2E
    hi = cs2.astype(BF16)
    r1 = cs2 - hi.astype(F32)
    mid = r1.astype(BF16)
    lo = (r1 - mid.astype(F32)).astype(BF16)
    feat = (_dot(hi, sel[0]) + _dot(mid, sel[1]) + _dot(lo, sel[2]) + ones).astype(BF16)
    return feat[:, 0:LANES], feat[:, LANES:2 * LANES]


def _fox_kernel(q_ref, k_ref, v_ref, gb_ref, qf_ref, kf_ref, z_ref, ka_scr, qt_scr, vt_scr, tri_scr, *, seq):
    tq = FOX_TQ
    hd = FOX_HEAD_DIM
    pair = pl.program_id(1)

    @pl.when(pair == 0)
    def _():
        ka_scr[:, LANES:2 * LANES] = kf_ref[...]
        key = lax.broadcasted_iota(jnp.int32, (tq, tq), 0)
        qry = lax.broadcasted_iota(jnp.int32, (tq, tq), 1)
        tri_scr[...] = jnp.where(qry >= key, 0.0, -jnp.inf)
        one_row = lax.broadcasted_iota(jnp.int32, (FOX_VROWS - hd, seq), 0) == 0
        for e in range(2):
            vt_scr[e, hd:FOX_VROWS, :] = jnp.where(one_row, 1.0, 0.0).astype(BF16)

    ka_scr[:, 0:LANES] = k_ref[...]
    sub = lax.broadcasted_iota(jnp.int32, (2 * LANES, 1), 0)
    keep = []
    for e in range(2):
        feat0 = LANES + FOX_FEAT * (2 * pair + e)
        own_q = (sub >= e * FOX_HEAD_DIM) & (sub < (e + 1) * FOX_HEAD_DIM)
        own_feat = (sub >= feat0) & (sub < feat0 + FOX_FEAT)
        keep.append(jnp.where(own_q | own_feat, 1.0, 0.0).astype(BF16))
    for blk in range(seq // tq):
        rows = slice(blk * tq, (blk + 1) * tq)
        q_aug_t = jnp.concatenate([q_ref[rows, :], qf_ref[rows, :]], axis=1).T
        v_t = v_ref[rows, :].T
        for e in range(2):
            qt_scr[e, :, rows] = q_aug_t * keep[e]
            vt_scr[e, 0:hd, rows] = v_t[e * hd:(e + 1) * hd, :]

    items = []
    for qi in range(seq // tq):
        q0 = qi * tq
        if FOX_TK <= tq:
            spans = [(k0, FOX_TK) for k0 in range(0, q0 + tq, FOX_TK)]
        else:
            spans = [(k0, FOX_TK) for k0 in range(0, q0 - FOX_TK + 1, FOX_TK)]
            k_done = len(spans) * FOX_TK
            spans.append((k_done, q0 + tq - k_done))
        for si, (k0, width) in enumerate(spans):
            for e in range(2):
                items.append((q0, k0, width, k0 + width > q0, e, si == 0, si == len(spans) - 1))
    n_items = len(items)

    def logits_of(idx):
        q0, k0, width, _, e, _, _ = items[idx]
        return _dot(ka_scr[k0:k0 + width, :], qt_scr[e, :, q0:q0 + tq])

    m = {}

    def softmax_of(idx, logit_t):
        q0, k0, width, masked, e, first, _ = items[idx]
        if masked:
            lo = max(q0 - k0, 0)
            diag = logit_t[lo:] + tri_scr[k0 + lo - q0:k0 + width - q0, :]
            logit_t = diag if lo == 0 else jnp.concatenate([logit_t[:lo], diag], axis=0)
        m_blk = jnp.max(logit_t, axis=0, keepdims=True)
        m_new = m_blk if first else jnp.maximum(m[e], m_blk)
        alpha = None if first else jnp.exp2(m[e] - m_new)
        m[e] = m_new
        return alpha, jnp.exp2(logit_t - m_new).astype(BF16)

    acc = {}
    logits = {i: logits_of(i) for i in range(min(2, n_items))}
    probs = {0: softmax_of(0, logits.pop(0))}
    for idx, (q0, k0, width, masked, e, first, last) in enumerate(items):
        if idx + 2 < n_items:
            logits[idx + 2] = logits_of(idx + 2)
        if idx + 1 < n_items:
            probs[idx + 1] = softmax_of(idx + 1, logits.pop(idx + 1))
        alpha, p_t = probs.pop(idx)
        pv = _dot(vt_scr[e, :, k0:k0 + width], p_t)
        acc[e] = pv if first else alpha * acc[e] + pv
        if last and e == 1:
            o_t = jnp.concatenate([acc[h][0:hd] / acc[h][hd:hd + 1] for h in range(2)], axis=0)
            z_ref[q0:q0 + tq, :] = (_sigmoid(gb_ref[q0:q0 + tq, :].astype(F32)) * o_t.T).astype(BF16)


def _fox_call(proj, q_feat, k_feat, batch, seq):
    n = batch * seq
    pairs = FOX_HEADS // 2
    per_group = D_MODEL // LANES

    def cols(group):
        return pl.BlockSpec((seq, LANES), lambda b, p: (b, group * per_group + p))

    feat = pl.BlockSpec((seq, LANES), lambda b, p: (b, 0))
    return pl.pallas_call(
        functools.partial(_fox_kernel, seq=seq),
        grid=(batch, pairs),
        in_specs=[
            cols(COL_FX_Q),
            cols(COL_FX_K),
            cols(COL_FX_V),
            cols(COL_G_B),
            feat,
            feat,
        ],
        out_specs=pl.BlockSpec((seq, LANES), lambda b, p: (b, p)),
        out_shape=jax.ShapeDtypeStruct((n, D_MODEL), BF16),
        scratch_shapes=[
            pltpu.VMEM((seq, 2 * LANES), BF16),
            pltpu.VMEM((2, 2 * LANES, seq), BF16),
            pltpu.VMEM((2, FOX_VROWS, seq), BF16),
            pltpu.VMEM((FOX_TQ, FOX_TQ), F32),
        ],
        compiler_params=pltpu.CompilerParams(
            dimension_semantics=("parallel", "arbitrary"), vmem_limit_bytes=VMEM_LIMIT),
        name="fox",
    )(proj, proj, proj, proj, q_feat, k_feat)


def _layer(x2d, batch, seq, ffn1_norm, ffn1_w_gate, ffn1_w_up, ffn1_w_down, mix_norm, w_in, b_in,
           conv_w, conv_b, ml_head_norm, w_out, ffn2_norm, ffn2_w_gate, ffn2_w_up, ffn2_w_down, final_g):
    d = D_MODEL
    x1, h1 = _ffn_call(x2d, ffn1_norm[None, :], ffn1_w_gate.astype(BF16), ffn1_w_up.astype(BF16),
                       ffn1_w_down.astype(BF16), next_g=mix_norm[None, :])

    sizes = (d, d, d, d, ML_HEADS, ML_HEADS, d, d, d, FOX_HEADS, d, d)
    starts = [0]
    for s in sizes[:-1]:
        starts.append(starts[-1] + s)
    (ml_q, ml_k, ml_v, ml_o, ml_i, ml_f, fx_q, fx_k, fx_v, fx_f, g_a, g_b) = [
        slice(a, a + s) for a, s in zip(starts, sizes)]
    order = [None] * N_COL_GROUPS
    for col, grp in ((COL_ML_Q, ml_q), (COL_ML_K, ml_k), (COL_ML_V, ml_v), (COL_ML_O, ml_o), (COL_G_A, g_a),
                     (COL_G_B, g_b), (COL_FX_Q, fx_q), (COL_FX_K, fx_k), (COL_FX_V, fx_v)):
        order[col] = grp
    w_t = w_in.T
    assert all(c.start % SUBLANES == 0 for c in order)
    w_main_t = _pack_w_call(w_t, jnp.asarray([c.start // SUBLANES for c in order], jnp.int32))
    b_main = jnp.concatenate([b_in[c] for c in order])[None, :]
    col_scale = jnp.ones((N_COL_GROUPS, d), F32).at[COL_FX_Q].set(LOG2E * FOX_HEAD_DIM ** -0.5).reshape(1, -1)
    pad = GATE_COLS - 2 * ML_HEADS - FOX_HEADS
    w_gates_t = jnp.concatenate([w_t[ml_i], w_t[ml_f], w_t[fx_f], jnp.zeros((pad, d), w_in.dtype)], axis=0)
    b_gates = jnp.concatenate([b_in[ml_i], b_in[ml_f], b_in[fx_f], jnp.zeros((pad,), b_in.dtype)])[None, :]
    proj, gates = _proj_call(h1, w_main_t, b_main, col_scale, w_gates_t, b_gates)

    z_a, q_feat, k_feat, (wg2, wu2, wd2, wo) = _mlstm_call(
        proj, gates, conv_w, conv_b[None, :], ml_head_norm[None, :], batch, seq,
        side_casts=(ffn2_w_gate, ffn2_w_up, ffn2_w_down, w_out))
    z_b = _fox_call(proj, q_feat, k_feat, batch, seq)

    return _ffn_call(x1, ffn2_norm[None, :], wg2, wu2, wd2, pre=(z_a, z_b, wo), final_g=final_g)


def kernel(x, ffn1_norm, ffn1_w_gate, ffn1_w_up, ffn1_w_down, mix_norm, w_in, b_in, conv_w, conv_b,
           ml_head_norm, w_out, ffn2_norm, ffn2_w_gate, ffn2_w_up, ffn2_w_down, final_norm):
    batch, seq, d = x.shape
    assert d == D_MODEL and ffn1_norm.shape[0] == 1, "single-layer configuration"
    out = _layer(x.reshape(batch * seq, d), batch, seq, ffn1_norm[0], ffn1_w_gate[0], ffn1_w_up[0],
                 ffn1_w_down[0], mix_norm[0], w_in[0], b_in[0], conv_w[0], conv_b[0], ml_head_norm[0],
                 w_out[0], ffn2_norm[0], ffn2_w_gate[0], ffn2_w_up[0], ffn2_w_down[0], final_norm[None, :])
    return out.reshape(batch, seq, d)
```

```python
import functools
import math

import jax
import jax.numpy as jnp
from jax import lax
from jax.experimental import pallas as pl
from jax.experimental.pallas import tpu as pltpu

D_MODEL = 1024
ML_HEADS = 4
ML_HEAD_DIM = 256
CONV_WIDTH = 4
FOX_HEADS = 16
FOX_HEAD_DIM = 64
D_FF = 2816
EPS = 1e-6

LANES = 128
SUBLANES = 8
GATE_COLS = LANES
ML_I0, ML_F0, FX_F0 = 0, ML_HEADS, 2 * ML_HEADS

(COL_ML_Q, COL_ML_K, COL_ML_V, COL_ML_O, COL_G_A, COL_G_B, COL_FX_Q, COL_FX_K, COL_FX_V) = range(9)
N_COL_GROUPS = 9
LOG2E = 1.4426950408889634
K_SCALE_LOG = 0.5 * math.log(ML_HEAD_DIM)

FFN1_TM = 512
FFN2_TM = 1024
FFN_SUB = 512
PACK_ROWS = 512
FFN_TF = 256
PROJ_TM = 1024
PROJ_TN = 4608
ML_CHUNK = 256
HALO = 16
ML_PAD = 16
FOX_TQ = 256
FOX_TK = 2048
FOX_FEAT = 8
FOX_VROWS = 80
VMEM_LIMIT = 56 * 1024 * 1024

F32 = jnp.float32
BF16 = jnp.bfloat16


def _dot(a, b):
    return jnp.dot(a, b, preferred_element_type=F32)


def _dot_nt(a, b):
    return lax.dot_general(a, b, (((1,), (1,)), ((), ())), preferred_element_type=F32)


def _rmsnorm(x, g):
    return x * lax.rsqrt(jnp.mean(x * x, axis=-1, keepdims=True) + EPS) * g


def _sigmoid(x):
    return 1.0 / (1.0 + jnp.exp2(x * (-LOG2E)))


def _log_sigmoid(x):
    return jnp.minimum(x, 0.0) - jnp.log(1.0 + jnp.exp(-jnp.abs(x)))


def _cumsum_rows(x, tril):
    hi = x.astype(BF16)
    r1 = x - hi.astype(F32)
    mid = r1.astype(BF16)
    lo = (r1 - mid.astype(F32)).astype(BF16)
    return _dot(tril, hi) + _dot(tril, mid) + _dot(tril, lo)


def _tril(n):
    r = lax.broadcasted_iota(jnp.int32, (n, n), 0)
    c = lax.broadcasted_iota(jnp.int32, (n, n), 1)
    return r >= c


def _ffn_kernel(*refs, tm, pre_proj, final_norm, next_norm, pack_steps):
    refs = list(refs)
    if pack_steps:
        refs.pop(0)
    x_ref = refs.pop(0)
    if pre_proj:
        za_ref, zb_ref, wo_ref = refs.pop(0), refs.pop(0), refs.pop(0)
    g_ref, wg_ref, wu_ref, wd_ref = refs.pop(0), refs.pop(0), refs.pop(0), refs.pop(0)
    if final_norm:
        gf_ref = refs.pop(0)
    if next_norm:
        gn_ref = refs.pop(0)
    if pack_steps:
        pack_in = refs.pop(0)
    o_ref = refs.pop(0)
    if next_norm:
        hn_ref = refs.pop(0)
    if pack_steps:
        pack_out = refs.pop(0)

        @pl.when(pl.program_id(0) < pack_steps)
        def _():
            pack_out[...] = pack_in[...].astype(BF16)
    (a_scr,) = refs

    def prologue(s):
        rows = slice(s * FFN_SUB, (s + 1) * FFN_SUB)
        x = x_ref[rows, :]
        if pre_proj:
            y = (za_ref[rows, :].astype(F32) + zb_ref[rows, :].astype(F32)).astype(BF16)
            x = x + _dot(y, wo_ref[...])
        return x, _rmsnorm(x, g_ref[...]).astype(BF16)

    def epilogue(s, x, y):
        rows = slice(s * FFN_SUB, (s + 1) * FFN_SUB)
        out = x + 0.5 * y
        if next_norm:
            hn_ref[rows, :] = _rmsnorm(out, gn_ref[...]).astype(BF16)
        if final_norm:
            out = _rmsnorm(out, gf_ref[...])
        o_ref[rows, :] = out

    n_sub = tm // FFN_SUB
    x, h = prologue(0)
    prev = None
    for s in range(n_sub):
        nxt = None
        for j in range(D_FF // FFN_TF):
            cols = slice(j * FFN_TF, (j + 1) * FFN_TF)
            gate = _dot(h, wg_ref[:, cols])
            up = _dot(h, wu_ref[:, cols])
            a_scr[s % 2, :, cols] = (gate * _sigmoid(gate) * up).astype(BF16)
            if j == 0 and prev is not None:
                prev = prev + (_dot(a_scr[(s - 1) % 2], wd_ref[...]),)
            if j == 1 and s + 1 < n_sub:
                nxt = prologue(s + 1)
            if j == 3 and prev is not None:
                epilogue(*prev)
                prev = None
        prev = (s, x)
        if nxt is not None:
            x, h = nxt
    epilogue(*prev, _dot(a_scr[(n_sub - 1) % 2], wd_ref[...]))


def _ffn_call(x, norm_g, wg, wu, wd, tm, pre=None, final_g=None, next_g=None, pack=None):
    n = x.shape[0]
    row = pl.BlockSpec((tm, D_MODEL), lambda i, *_: (i, 0))
    vec = pl.BlockSpec((1, D_MODEL), lambda i, *_: (0, 0))
    out_specs, out_shape = [row], [jax.ShapeDtypeStruct((n, D_MODEL), F32)]
    if next_g is not None:
        out_specs.append(row)
        out_shape.append(jax.ShapeDtypeStruct((n, D_MODEL), BF16))

    def resident(shape):
        return pl.BlockSpec(shape, lambda i, *_: (0, 0), pipeline_mode=pl.Buffered(1))

    args, specs = [x], [row]
    if pre is not None:
        za, zb, wo = pre
        args += [za, zb, wo]
        specs += [row, row, resident((D_MODEL, D_MODEL))]
    args += [norm_g, wg, wu, wd]
    specs += [vec, resident((D_MODEL, D_FF)), resident((D_MODEL, D_FF)), resident((D_FF, D_MODEL))]
    for g in (final_g, next_g):
        if g is not None:
            args.append(g)
            specs.append(vec)
    prefetch, pack_steps = [], 0
    if pack is not None:
        w_t, starts = pack
        pack_steps = starts.shape[0]
        assert pack_steps <= n // tm
        last = pack_steps - 1
        specs.append(pl.BlockSpec((pl.Element(PACK_ROWS), pl.Element(D_MODEL)),
                                  lambda i, s: (s[jnp.minimum(i, last)] * SUBLANES, 0)))
        args.append(w_t)
        out_specs.append(pl.BlockSpec((PACK_ROWS, D_MODEL), lambda i, s: (jnp.minimum(i, last), 0)))
        out_shape.append(jax.ShapeDtypeStruct((pack_steps * PACK_ROWS, D_MODEL), BF16))
        prefetch = [starts]
    return pl.pallas_call(
        functools.partial(_ffn_kernel, tm=tm, pre_proj=pre is not None, final_norm=final_g is not None,
                          next_norm=next_g is not None, pack_steps=pack_steps),
        grid_spec=pltpu.PrefetchScalarGridSpec(
            num_scalar_prefetch=len(prefetch),
            grid=(n // tm,),
            in_specs=specs,
            out_specs=out_specs,
            scratch_shapes=[pltpu.VMEM((2, FFN_SUB, D_FF), BF16)],
        ),
        out_shape=out_shape,
        compiler_params=pltpu.CompilerParams(
            dimension_semantics=("arbitrary",), vmem_limit_bytes=VMEM_LIMIT),
        name="ffn2" if pre is not None else "ffn1",
    )(*prefetch, *args)


def _proj_kernel(h_ref, w_ref, b_ref, s_ref, wgt_ref, bgt_ref, p_ref, gt_ref):
    h = h_ref[...]

    @pl.when(pl.program_id(1) == 0)
    def _():
        gt_ref[...] = _dot_nt(h, wgt_ref[...].astype(BF16)) + bgt_ref[...]

    p_ref[...] = ((_dot_nt(h, w_ref[...]) + b_ref[...]) * s_ref[...]).astype(BF16)


def _proj_call(h, w_t, b, col_scale, w_gates_t, b_gates):
    n = h.shape[0]
    n_out = w_t.shape[0]
    return pl.pallas_call(
        _proj_kernel,
        grid=(n // PROJ_TM, n_out // PROJ_TN),
        in_specs=[
            pl.BlockSpec((PROJ_TM, D_MODEL), lambda i, j: (i, 0)),
            pl.BlockSpec((PROJ_TN, D_MODEL), lambda i, j: (j, 0)),
            pl.BlockSpec((1, PROJ_TN), lambda i, j: (0, j)),
            pl.BlockSpec((1, PROJ_TN), lambda i, j: (0, j)),
            pl.BlockSpec((GATE_COLS, D_MODEL), lambda i, j: (0, 0)),
            pl.BlockSpec((1, GATE_COLS), lambda i, j: (0, 0)),
        ],
        out_specs=[
            pl.BlockSpec((PROJ_TM, PROJ_TN), lambda i, j: (i, j)),
            pl.BlockSpec((PROJ_TM, GATE_COLS), lambda i, j: (i, 0)),
        ],
        out_shape=[
            jax.ShapeDtypeStruct((n, n_out), BF16),
            jax.ShapeDtypeStruct((n, GATE_COLS), F32),
        ],
        compiler_params=pltpu.CompilerParams(
            dimension_semantics=("parallel", "arbitrary"), vmem_limit_bytes=VMEM_LIMIT),
        name="in_proj",
    )(h, w_t, b, col_scale, w_gates_t, b_gates)


def _mlstm_kernel(*refs, cast_steps):
    n_cast = len(cast_steps)
    ml_in, refs = refs[:8], refs[8:]
    cast_in, refs = refs[:n_cast], refs[n_cast:]
    ml_out, refs = refs[:3], refs[3:]
    cast_out, scratch = refs[:n_cast], refs[n_cast:]
    step = pl.program_id(0) * pl.num_programs(1) + pl.program_id(1)
    for src, dst, n_steps in zip(cast_in, cast_out, cast_steps):
        @pl.when(step < n_steps)
        def _(src=src, dst=dst):
            dst[...] = src[...].astype(BF16)
    _mlstm_chunk(*ml_in, *ml_out, *scratch)


def _mlstm_chunk(qk_ref, o_ref, ga_ref, v_ref, gt_ref, cw_ref, cb_ref, ln_ref, z_ref, qf_ref, kf_ref,
                 halo_scr, c_scr, m_scr, cum_scr):
    L = ML_CHUNK
    dh = ML_HEAD_DIM

    @pl.when(pl.program_id(1) == 0)
    def _():
        halo_scr[...] = jnp.zeros_like(halo_scr)
        c_scr[...] = jnp.zeros_like(c_scr)
        m_scr[...] = jnp.zeros_like(m_scr)
        cum_scr[...] = jnp.zeros_like(cum_scr)

    u16 = qk_ref[...]
    halo16 = halo_scr[...]
    t_out = lax.broadcasted_iota(jnp.int32, (L, L), 0)
    t_in = lax.broadcasted_iota(jnp.int32, (L, L), 1)
    h_out = lax.broadcasted_iota(jnp.int32, (HALO, HALO), 0)
    h_in = lax.broadcasted_iota(jnp.int32, (HALO, HALO), 1)
    y = cb_ref[...] + cw_ref[CONV_WIDTH - 1:CONV_WIDTH, :] * u16.astype(F32)
    head_fix = jnp.zeros((HALO, 2 * D_MODEL), F32)
    for d in range(1, CONV_WIDTH):
        w_d = cw_ref[CONV_WIDTH - 1 - d:CONV_WIDTH - d, :]
        y = y + w_d * _dot(jnp.where(t_in == t_out - d, 1.0, 0.0).astype(BF16), u16)
        head_fix = head_fix + w_d * _dot(jnp.where(h_in == h_out - d + HALO, 1.0, 0.0).astype(BF16), halo16)
    y = jnp.concatenate([y[0:HALO] + head_fix, y[HALO:]], axis=0)
    halo_scr[...] = u16[L - HALO:L, :]
    qk = y * _sigmoid(y)

    gates = gt_ref[...]
    tril = _tril(L)
    bcum = _cumsum_rows(_log_sigmoid(gates), tril.astype(BF16))
    gates_t = gates.T
    bcum_t = bcum.T
    cum = bcum + cum_scr[...]
    cum_scr[...] = cum[L - 1:L, :]
    qf_ref[...], kf_ref[...] = _fox_bias_features(cum)
    pad_row = lax.broadcasted_iota(jnp.int32, (ML_PAD, L), 0)
    ones_pad = jnp.where(pad_row == 0, 1.0, 0.0).astype(BF16)

    def matmul_stage(h):
        q_t = qk[:, h * dh:(h + 1) * dh].astype(BF16).T
        k16 = qk[:, D_MODEL + h * dh:D_MODEL + (h + 1) * dh].astype(BF16)
        state = c_scr[h]
        return q_t, k16, state, _dot(k16, q_t), _dot(state.astype(BF16), q_t)

    staged = matmul_stage(0)
    for h in range(ML_HEADS):
        cols = slice(h * dh, (h + 1) * dh)
        q_t, k16, state, qk_t, inter_t = staged
        if h + 1 < ML_HEADS:
            staged = matmul_stage(h + 1)
        v_aug = jnp.concatenate([v_ref[:, cols].T, ones_pad], axis=0)
        li_row = gates_t[ML_I0 + h:ML_I0 + h + 1, :]
        b_row = bcum_t[ML_F0 + h:ML_F0 + h + 1, :]
        ib_col = gates[:, ML_I0 + h:ML_I0 + h + 1] - bcum[:, ML_F0 + h:ML_F0 + h + 1]
        m_prev = m_scr[h][:, 0:1]

        dmat_t = jnp.where(t_out <= t_in, b_row + ib_col, -jnp.inf)
        inter = b_row + m_prev
        m_t = jnp.maximum(jnp.max(dmat_t, axis=0, keepdims=True), inter)
        w_inter = jnp.exp(inter - m_t)
        s_t = (qk_t * jnp.exp(dmat_t - (m_t + K_SCALE_LOG))).astype(BF16)
        num_den = _dot(v_aug, s_t) + w_inter * inter_t
        den = num_den[dh:dh + 1, :]
        hh_t = num_den[0:dh, :] / jnp.maximum(jnp.abs(den), jnp.exp(-m_t))

        b_last = b_row[:, L - 1:L]
        log_w = b_last - b_row + li_row
        m_new = jnp.maximum(b_last + m_prev, jnp.max(log_w, axis=1, keepdims=True))
        decay = jnp.exp(b_last + m_prev - m_new)
        w = jnp.exp(log_w - m_new) * (dh ** -0.5)
        c_scr[h] = decay * state + _dot((v_aug.astype(F32) * w).astype(BF16), k16)
        m_scr[h] = jnp.broadcast_to(m_new, (1, LANES))

        mu = jnp.mean(hh_t, axis=0, keepdims=True)
        dlt = hh_t - mu
        var = jnp.mean(dlt * dlt, axis=0, keepdims=True)
        hn = (dlt * lax.rsqrt(var + EPS)).T * ln_ref[:, cols]
        gate = _sigmoid(ga_ref[:, cols].astype(F32)) * _sigmoid(o_ref[:, cols].astype(F32))
        z_ref[:, cols] = (gate * hn).astype(BF16)


def _mlstm_call(proj, gates, conv_w, conv_b, ln_g, batch, seq, side_casts):
    n = batch * seq
    nc = seq // ML_CHUNK
    L = ML_CHUNK
    total_steps = batch * nc

    def row(width, col):
        return pl.BlockSpec((L, width), lambda b, c: (b * nc + c, col))

    def full(shape):
        return pl.BlockSpec(shape, lambda b, c: (0, 0))

    cast_specs, cast_shapes, cast_steps = [], [], []
    for w in side_casts:
        rows, width = w.shape
        n_steps = max(s for s in range(1, total_steps + 1) if rows % s == 0 and (rows // s) % (2 * SUBLANES) == 0)
        spec = pl.BlockSpec((rows // n_steps, width), lambda b, c, last=n_steps - 1: (jnp.minimum(b * nc + c, last), 0))
        cast_specs.append(spec)
        cast_shapes.append(jax.ShapeDtypeStruct(w.shape, BF16))
        cast_steps.append(n_steps)

    feat_shape = jax.ShapeDtypeStruct((n, LANES), BF16)
    z_a, q_feat, k_feat, *casted = pl.pallas_call(
        functools.partial(_mlstm_kernel, cast_steps=tuple(cast_steps)),
        grid=(batch, nc),
        in_specs=[
            row(2 * D_MODEL, COL_ML_Q // 2),
            row(D_MODEL, COL_ML_O),
            row(D_MODEL, COL_G_A),
            row(D_MODEL, COL_ML_V),
            row(GATE_COLS, 0),
            full((CONV_WIDTH, 2 * D_MODEL)),
            full((1, 2 * D_MODEL)),
            full((1, D_MODEL)),
            *cast_specs,
        ],
        out_specs=[row(D_MODEL, 0), row(LANES, 0), row(LANES, 0), *cast_specs],
        out_shape=[jax.ShapeDtypeStruct((n, D_MODEL), BF16), feat_shape, feat_shape, *cast_shapes],
        scratch_shapes=[
            pltpu.VMEM((HALO, 2 * D_MODEL), BF16),
            pltpu.VMEM((ML_HEADS, ML_HEAD_DIM + ML_PAD, ML_HEAD_DIM), F32),
            pltpu.VMEM((ML_HEADS, 1, LANES), F32),
            pltpu.VMEM((1, GATE_COLS), F32),
        ],
        compiler_params=pltpu.CompilerParams(
            dimension_semantics=("arbitrary", "arbitrary"), vmem_limit_bytes=VMEM_LIMIT),
        name="mlstm",
    )(proj, proj, proj, proj, gates, conv_w, conv_b, ln_g, *side_casts)
    return z_a, q_feat, k_feat, casted


def _fox_bias_features(cs):
    r = lax.broadcasted_iota(jnp.int32, (GATE_COLS, 2 * LANES), 0)
    c = lax.broadcasted_iota(jnp.int32, (GATE_COLS, 2 * LANES), 1)
    head = r - FX_F0
    is_head = (head >= 0) & (head < FOX_HEADS)
    sel = []
    for piece in range(3):
        q_hit = is_head & (c == FOX_FEAT * head + piece)
        k_hit = is_head & (c == LANES + FOX_FEAT * head + 3 + piece)
        sel.append(jnp.where(q_hit, 1.0, jnp.where(k_hit, -1.0, 0.0)).astype(BF16))
    lane = lax.broadcasted_iota(jnp.int32, (1, 2 * LANES), 1)
    slot = lane % FOX_FEAT
    in_use = (lane % LANES) < FOX_FEAT * FOX_HEADS
    q_one = (lane < LANES) & (slot >= 3) & (slot < 6)
    k_one = (lane >= LANES) & (slot < 3)
    ones = jnp.where(in_use & (q_one | k_one), 1.0, 0.0)

    cs2 = cs * LOG2E
    hi = cs2.astype(BF16)
    r1 = cs2 - hi.astype(F32)
    mid = r1.astype(BF16)
    lo = (r1 - mid.astype(F32)).astype(BF16)
    feat = (_dot(hi, sel[0]) + _dot(mid, sel[1]) + _dot(lo, sel[2]) + ones).astype(BF16)
    return feat[:, 0:LANES], feat[:, LANES:2 * LANES]


def _fox_kernel(q_ref, k_ref, v_ref, gb_ref, qf_ref, kf_ref, z_ref, ka_scr, qt_scr, vt_scr, tri_scr, *, seq):
    tq = FOX_TQ
    hd = FOX_HEAD_DIM
    pair = pl.program_id(1)

    @pl.when(pair == 0)
    def _():
        ka_scr[:, LANES:2 * LANES] = kf_ref[...]
        key = lax.broadcasted_iota(jnp.int32, (tq, tq), 0)
        qry = lax.broadcasted_iota(jnp.int32, (tq, tq), 1)
        tri_scr[...] = jnp.where(qry >= key, 0.0, -jnp.inf)
        one_row = lax.broadcasted_iota(jnp.int32, (FOX_VROWS - hd, seq), 0) == 0
        for e in range(2):
            vt_scr[e, hd:FOX_VROWS, :] = jnp.where(one_row, 1.0, 0.0).astype(BF16)

    ka_scr[:, 0:LANES] = k_ref[...]
    sub = lax.broadcasted_iota(jnp.int32, (2 * LANES, 1), 0)
    keep = []
    for e in range(2):
        feat0 = LANES + FOX_FEAT * (2 * pair + e)
        own_q = (sub >= e * FOX_HEAD_DIM) & (sub < (e + 1) * FOX_HEAD_DIM)
        own_feat = (sub >= feat0) & (sub < feat0 + FOX_FEAT)
        keep.append(jnp.where(own_q | own_feat, 1.0, 0.0).astype(BF16))
    for blk in range(seq // tq):
        rows = slice(blk * tq, (blk + 1) * tq)
        q_aug_t = jnp.concatenate([q_ref[rows, :], qf_ref[rows, :]], axis=1).T
        v_t = v_ref[rows, :].T
        for e in range(2):
            qt_scr[e, :, rows] = q_aug_t * keep[e]
            vt_scr[e, 0:hd, rows] = v_t[e * hd:(e + 1) * hd, :]

    items = []
    for qi in range(seq // tq):
        q0 = qi * tq
        if FOX_TK <= tq:
            spans = [(k0, FOX_TK) for k0 in range(0, q0 + tq, FOX_TK)]
        else:
            spans = [(k0, FOX_TK) for k0 in range(0, q0 - FOX_TK + 1, FOX_TK)]
            k_done = len(spans) * FOX_TK
            spans.append((k_done, q0 + tq - k_done))
        for si, (k0, width) in enumerate(spans):
            for e in range(2):
                items.append((q0, k0, width, k0 + width > q0, e, si == 0, si == len(spans) - 1))
    n_items = len(items)

    def logits_of(idx):
        q0, k0, width, _, e, _, _ = items[idx]
        return _dot(ka_scr[k0:k0 + width, :], qt_scr[e, :, q0:q0 + tq])

    m = {}

    def softmax_of(idx, logit_t):
        q0, k0, width, masked, e, first, _ = items[idx]
        if masked:
            lo = max(q0 - k0, 0)
            diag = logit_t[lo:] + tri_scr[k0 + lo - q0:k0 + width - q0, :]
            logit_t = diag if lo == 0 else jnp.concatenate([logit_t[:lo], diag], axis=0)
        m_blk = jnp.max(logit_t, axis=0, keepdims=True)
        m_new = m_blk if first else jnp.maximum(m[e], m_blk)
        alpha = None if first else jnp.exp2(m[e] - m_new)
        m[e] = m_new
        return alpha, jnp.exp2(logit_t - m_new).astype(BF16)

    acc = {}
    logits = {i: logits_of(i) for i in range(min(2, n_items))}
    probs = {0: softmax_of(0, logits.pop(0))}
    for idx, (q0, k0, width, masked, e, first, last) in enumerate(items):
        if idx + 2 < n_items:
            logits[idx + 2] = logits_of(idx + 2)
        if idx + 1 < n_items:
            probs[idx + 1] = softmax_of(idx + 1, logits.pop(idx + 1))
        alpha, p_t = probs.pop(idx)
        pv = _dot(vt_scr[e, :, k0:k0 + width], p_t)
        acc[e] = pv if first else alpha * acc[e] + pv
        if last and e == 1:
            o_t = jnp.concatenate([acc[h][0:hd] / acc[h][hd:hd + 1] for h in range(2)], axis=0)
            z_ref[q0:q0 + tq, :] = (_sigmoid(gb_ref[q0:q0 + tq, :].astype(F32)) * o_t.T).astype(BF16)


def _fox_call(proj, q_feat, k_feat, batch, seq):
    n = batch * seq
    pairs = FOX_HEADS // 2
    per_group = D_MODEL // LANES

    def cols(group):
        return pl.BlockSpec((seq, LANES), lambda b, p: (b, group * per_group + p))

    feat = pl.BlockSpec((seq, LANES), lambda b, p: (b, 0))
    return pl.pallas_call(
        functools.partial(_fox_kernel, seq=seq),
        grid=(batch, pairs),
        in_specs=[
            cols(COL_FX_Q),
            cols(COL_FX_K),
            cols(COL_FX_V),
            cols(COL_G_B),
            feat,
            feat,
        ],
        out_specs=pl.BlockSpec((seq, LANES), lambda b, p: (b, p)),
        out_shape=jax.ShapeDtypeStruct((n, D_MODEL), BF16),
        scratch_shapes=[
            pltpu.VMEM((seq, 2 * LANES), BF16),
            pltpu.VMEM((2, 2 * LANES, seq), BF16),
            pltpu.VMEM((2, FOX_VROWS, seq), BF16),
            pltpu.VMEM((FOX_TQ, FOX_TQ), F32),
        ],
        compiler_params=pltpu.CompilerParams(
            dimension_semantics=("parallel", "arbitrary"), vmem_limit_bytes=VMEM_LIMIT),
        name="fox",
    )(proj, proj, proj, proj, q_feat, k_feat)


def _layer(x2d, batch, seq, ffn1_norm, ffn1_w_gate, ffn1_w_up, ffn1_w_down, mix_norm, w_in, b_in,
           conv_w, conv_b, ml_head_norm, w_out, ffn2_norm, ffn2_w_gate, ffn2_w_up, ffn2_w_down, final_g):
    d = D_MODEL
    sizes = (d, d, d, d, ML_HEADS, ML_HEADS, d, d, d, FOX_HEADS, d, d)
    starts = [0]
    for s in sizes[:-1]:
        starts.append(starts[-1] + s)
    (ml_q, ml_k, ml_v, ml_o, ml_i, ml_f, fx_q, fx_k, fx_v, fx_f, g_a, g_b) = [
        slice(a, a + s) for a, s in zip(starts, sizes)]
    order = [None] * N_COL_GROUPS
    for col, grp in ((COL_ML_Q, ml_q), (COL_ML_K, ml_k), (COL_ML_V, ml_v), (COL_ML_O, ml_o), (COL_G_A, g_a),
                     (COL_G_B, g_b), (COL_FX_Q, fx_q), (COL_FX_K, fx_k), (COL_FX_V, fx_v)):
        order[col] = grp
    w_t = w_in.T
    window_starts = [c.start + r for c in order for r in range(0, d, PACK_ROWS)]
    assert all(s % SUBLANES == 0 for s in window_starts)
    x1, h1, w_main_t = _ffn_call(
        x2d, ffn1_norm[None, :], ffn1_w_gate.astype(BF16), ffn1_w_up.astype(BF16), ffn1_w_down.astype(BF16),
        FFN1_TM, next_g=mix_norm[None, :],
        pack=(w_t, jnp.asarray([s // SUBLANES for s in window_starts], jnp.int32)))
    b_main = jnp.concatenate([b_in[c] for c in order])[None, :]
    col_scale = jnp.ones((N_COL_GROUPS, d), F32).at[COL_FX_Q].set(LOG2E * FOX_HEAD_DIM ** -0.5).reshape(1, -1)
    pad = GATE_COLS - 2 * ML_HEADS - FOX_HEADS
    w_gates_t = jnp.concatenate([w_t[ml_i], w_t[ml_f], w_t[fx_f], jnp.zeros((pad, d), w_in.dtype)], axis=0)
    b_gates = jnp.concatenate([b_in[ml_i], b_in[ml_f], b_in[fx_f], jnp.zeros((pad,), b_in.dtype)])[None, :]
    proj, gates = _proj_call(h1, w_main_t, b_main, col_scale, w_gates_t, b_gates)

    z_a, q_feat, k_feat, (wg2, wu2, wd2, wo) = _mlstm_call(
        proj, gates, conv_w, conv_b[None, :], ml_head_norm[None, :], batch, seq,
        side_casts=(ffn2_w_gate, ffn2_w_up, ffn2_w_down, w_out))
    z_b = _fox_call(proj, q_feat, k_feat, batch, seq)

    (out,) = _ffn_call(x1, ffn2_norm[None, :], wg2, wu2, wd2, FFN2_TM, pre=(z_a, z_b, wo), final_g=final_g)
    return out


def kernel(x, ffn1_norm, ffn1_w_gate, ffn1_w_up, ffn1_w_down, mix_norm, w_in, b_in, conv_w, conv_b,
           ml_head_norm, w_out, ffn2_norm, ffn2_w_gate, ffn2_w_up, ffn2_w_down, final_norm):
    batch, seq, d = x.shape
    assert d == D_MODEL and ffn1_norm.shape[0] == 1, "single-layer configuration"
    out = _layer(x.reshape(batch * seq, d), batch, seq, ffn1_norm[0], ffn1_w_gate[0], ffn1_w_up[0],
                 ffn1_w_down[0], mix_norm[0], w_in[0], b_in[0], conv_w[0], conv_b[0], ml_head_norm[0],
                 w_out[0], ffn2_norm[0], ffn2_w_gate[0], ffn2_w_up[0], ffn2_w_down[0], final_norm[None, :])
    return out.reshape(batch, seq, d)
```
